```python
import math
import jax, jax.numpy as jnp
from jax import lax
import numpy as np

D_MODEL = 2048
BATCH = 8
SEQ = 4096
DEPTH = 2
DEC_BATCH = 2
DEC_SEQ = 16384
PAST_LEN = 128

MIX_WIDTH = D_MODEL
DIFF_WIDTH = MIX_WIDTH // 2
NA_WIDTH = MIX_WIDTH - DIFF_WIDTH
DIFF_HEAD_DIM = 64
DIFF_HEADS = DIFF_WIDTH // (2 * DIFF_HEAD_DIM)
NA_HEAD_DIM = 128
NA_HEADS = NA_WIDTH // NA_HEAD_DIM
D_FF = 5632
GRID_W = 64
NA_KH_MAX = 8
NA_KW = 16
Q_BLOCK = 128
ROPE_THETA = 10000.0
LN_EPS = 1e-5
RMS_EPS = 1e-5
DEEPNORM_ALPHA = (2.0 * DEPTH) ** 0.25
DEEPNORM_BETA = (8.0 * DEPTH) ** -0.25
DQ_COLS = DIFF_HEADS * 2 * DIFF_HEAD_DIM
DK_COLS = DIFF_HEADS * 2 * DIFF_HEAD_DIM
DV_COLS = DIFF_HEADS * 2 * DIFF_HEAD_DIM
NQ_COLS = NA_HEADS * NA_HEAD_DIM
NK_COLS = NA_HEADS * NA_HEAD_DIM
NV_COLS = NA_HEADS * NA_HEAD_DIM
IN_COLS = DQ_COLS + DK_COLS + DV_COLS + NQ_COLS + NK_COLS + NV_COLS

kernel_name = "hybrid_diffattn_natten_macaron_encoder"


def layer_norm(x, g, b):
    xf = x.astype(jnp.float32)
    mu = jnp.mean(xf, axis=-1, keepdims=True)
    var = jnp.mean(jnp.square(xf - mu), axis=-1, keepdims=True)
    y = (xf - mu) * lax.rsqrt(var + LN_EPS) * g.astype(jnp.float32) + b.astype(jnp.float32)
    return y.astype(x.dtype)


def modulate(x, shift, scale):
    return x * (1 + scale[:, None, :]) + shift[:, None, :]


def swiglu(h, w_gu, w_down):
    gate, up = jnp.split(h @ w_gu, 2, axis=-1)
    return (jax.nn.silu(gate) * up) @ w_down


def rotary_tables(seq_len, dim):
    inv_freq = ROPE_THETA ** (-jnp.arange(0, dim, 2, dtype=jnp.float32) / dim)
    ang = jnp.arange(seq_len, dtype=jnp.float32)[:, None] * inv_freq[None, :]
    ang = jnp.concatenate([ang, ang], axis=-1)
    return jnp.cos(ang), jnp.sin(ang)


def apply_rotary(x, cos, sin):
    c = cos[None, :, None, None, :]
    s = sin[None, :, None, None, :]
    x1, x2 = jnp.split(x, 2, axis=-1)
    rot = jnp.concatenate([-x2, x1], axis=-1)
    return (x * c + rot * s).astype(x.dtype)


def diff_attention(q, k, v, lam, norm_g, lam_init):
    B, T, H, _, dh = q.shape
    nqb = T // Q_BLOCK
    qb = q.reshape(B, nqb, Q_BLOCK, H, 2, dh).transpose(1, 0, 2, 3, 4, 5)
    scale = dh ** -0.5

    def one_block(q_blk):
        s = jnp.einsum('bqhcd,bkhcd->bhcqk', q_blk, k).astype(jnp.float32) * scale
        p = jax.nn.softmax(s, axis=-1)
        a = p[:, :, 0] - lam * p[:, :, 1]
        return jnp.einsum('bhqk,bkhe->bqhe', a.astype(v.dtype), v)

    o = lax.map(one_block, qb)
    o = o.transpose(1, 0, 2, 3, 4).reshape(B, T, H, 2 * dh)
    of = o.astype(jnp.float32)
    of = of * lax.rsqrt(jnp.mean(jnp.square(of), axis=-1, keepdims=True) + RMS_EPS)
    of = of * norm_g.astype(jnp.float32) * (1.0 - lam_init)
    return of.astype(v.dtype).reshape(B, T, H * 2 * dh)


def neighborhood_attention(q, k, v, rpb):
    B, T, H, d = q.shape
    rows = T // GRID_W
    kh = min(NA_KH_MAX, rows)
    kg = k.reshape(B, rows, GRID_W, H, d)
    vg = v.reshape(B, rows, GRID_W, H, d)
    qg = q.reshape(B, rows, GRID_W, H, d).transpose(1, 0, 2, 3, 4)
    qc = jnp.arange(GRID_W)
    kc = jnp.arange(GRID_W)
    cs = jnp.clip(qc - NA_KW // 2, 0, GRID_W - NA_KW)
    col_valid = (kc[None, :] >= cs[:, None]) & (kc[None, :] < cs[:, None] + NA_KW)
    col_idx = jnp.clip(kc[None, :] - qc[:, None], -(NA_KW - 1), NA_KW - 1) + (NA_KW - 1)
    rpb_cols = rpb[:, :, col_idx]
    scale = d ** -0.5

    def one_row(args):
        r, q_row = args
        rs = jnp.clip(r - kh // 2, 0, rows - kh)
        k_rows = lax.dynamic_slice_in_dim(kg, rs, kh, axis=1)
        v_rows = lax.dynamic_slice_in_dim(vg, rs, kh, axis=1)
        row_idx = rs + jnp.arange(kh) - r + (NA_KH_MAX - 1)
        bias = jnp.take(rpb_cols, row_idx, axis=1).transpose(0, 2, 1, 3)
        s = jnp.einsum('bqhd,bjkhd->bhqjk', q_row, k_rows).astype(jnp.float32) * scale
        s = s + bias.astype(jnp.float32)[None]
        s = jnp.where(col_valid[None, None, :, None, :], s, -jnp.inf)
        p = jax.nn.softmax(s.reshape(B, H, GRID_W, kh * GRID_W), axis=-1)
        p = p.reshape(B, H, GRID_W, kh, GRID_W)
        return jnp.einsum('bhqjk,bjkhd->bqhd', p.astype(v.dtype), v_rows)

    o = lax.map(one_row, (jnp.arange(rows), qg))
    return o.transpose(1, 0, 2, 3, 4).reshape(B, T, H * d)


def encoder_layer(x, c, layer, w_ada, b_ada, ln_g, ln_b, ffn1_w_gu, ffn1_w_down,
                  w_in, w_out, lam_q1, lam_k1, lam_q2, lam_k2, diff_norm_g, rpb,
                  ffn2_w_gu, ffn2_w_down):
    B, T, _ = x.shape
    mod = jax.nn.silu(c) @ w_ada + b_ada
    s1, m1, g1, s2, m2, g2, s3, m3, g3 = jnp.split(mod, 9, axis=-1)

    h = modulate(x, s1, m1)
    f = swiglu(h, ffn1_w_gu, ffn1_w_down)
    x = layer_norm(DEEPNORM_ALPHA * x + (1 + g1[:, None, :]) * 0.5 * f, ln_g[0], ln_b[0])

    h = modulate(x, s2, m2)
    proj = h @ w_in
    splits = np.cumsum([DQ_COLS, DK_COLS, DV_COLS, NQ_COLS, NK_COLS])
    dq, dk, dv, nq, nk, nv = jnp.split(proj, splits, axis=-1)
    dq = dq.reshape(B, T, DIFF_HEADS, 2, DIFF_HEAD_DIM)
    dk = dk.reshape(B, T, DIFF_HEADS, 2, DIFF_HEAD_DIM)
    dv = dv.reshape(B, T, DIFF_HEADS, 2 * DIFF_HEAD_DIM)
    cos, sin = rotary_tables(T, DIFF_HEAD_DIM)
    dq = apply_rotary(dq, cos, sin)
    dk = apply_rotary(dk, cos, sin)
    lam_init = 0.8 - 0.6 * math.exp(-0.3 * layer)
    lam = (jnp.exp(jnp.sum(lam_q1.astype(jnp.float32) * lam_k1.astype(jnp.float32)))
           - jnp.exp(jnp.sum(lam_q2.astype(jnp.float32) * lam_k2.astype(jnp.float32)))
           + lam_init)
    o_diff = diff_attention(dq, dk, dv, lam, diff_norm_g, lam_init)
    nq = nq.reshape(B, T, NA_HEADS, NA_HEAD_DIM)
    nk = nk.reshape(B, T, NA_HEADS, NA_HEAD_DIM)
    nv = nv.reshape(B, T, NA_HEADS, NA_HEAD_DIM)
    o_na = neighborhood_attention(nq, nk, nv, rpb)
    a = jnp.concatenate([o_diff, o_na], axis=-1) @ w_out
    x = layer_norm(DEEPNORM_ALPHA * x + (1 + g2[:, None, :]) * a, ln_g[1], ln_b[1])

    h = modulate(x, s3, m3)
    f = swiglu(h, ffn2_w_gu, ffn2_w_down)
    x = layer_norm(DEEPNORM_ALPHA * x + (1 + g3[:, None, :]) * 0.5 * f, ln_g[2], ln_b[2])
    return x


def setup_inputs(seed: int = 0) -> dict:
    key = jax.random.key(seed)
    ks = jax.random.split(key, 24)
    f32 = jnp.float32
    n = lambda k, shape, s: (jax.random.normal(k, shape, f32) * s)
    return {
        "x_prompt": n(ks[0], (BATCH, SEQ, D_MODEL), 1.0),
        "x_sample": n(ks[1], (DEC_BATCH, DEC_SEQ, D_MODEL), 1.0),
        "c_prompt": n(ks[2], (BATCH, D_MODEL), 1.0),
        "c_sample": n(ks[3], (DEC_BATCH, D_MODEL), 1.0),
        "w_ada": n(ks[4], (DEPTH, D_MODEL, 9 * D_MODEL), 0.1 * D_MODEL ** -0.5),
        "b_ada": n(ks[5], (DEPTH, 9 * D_MODEL), 0.01),
        "ln_g": 1.0 + n(ks[6], (DEPTH, 3, D_MODEL), 0.02),
        "ln_b": n(ks[7], (DEPTH, 3, D_MODEL), 0.02),
        "ffn1_w_gu": n(ks[8], (DEPTH, D_MODEL, 2 * D_FF), D_MODEL ** -0.5),
        "ffn1_w_down": n(ks[9], (DEPTH, D_FF, D_MODEL), DEEPNORM_BETA * D_FF ** -0.5),
        "w_in": n(ks[10], (DEPTH, D_MODEL, IN_COLS), D_MODEL ** -0.5),
        "w_out": n(ks[11], (DEPTH, MIX_WIDTH, D_MODEL), DEEPNORM_BETA * MIX_WIDTH ** -0.5),
        "lam_q1": n(ks[12], (DEPTH, DIFF_HEAD_DIM), 0.1),
        "lam_k1": n(ks[13], (DEPTH, DIFF_HEAD_DIM), 0.1),
        "lam_q2": n(ks[14], (DEPTH, DIFF_HEAD_DIM), 0.1),
        "lam_k2": n(ks[15], (DEPTH, DIFF_HEAD_DIM), 0.1),
        "diff_norm_g": 1.0 + n(ks[16], (DEPTH, 2 * DIFF_HEAD_DIM), 0.02),
        "rpb": n(ks[17], (DEPTH, NA_HEADS, 2 * NA_KH_MAX - 1, 2 * NA_KW - 1), 0.1),
        "ffn2_w_gu": n(ks[18], (DEPTH, D_MODEL, 2 * D_FF), D_MODEL ** -0.5),
        "ffn2_w_down": n(ks[19], (DEPTH, D_FF, D_MODEL), DEEPNORM_BETA * D_FF ** -0.5),
    }


def reference(x_prompt, x_sample, c_prompt, c_sample, w_ada, b_ada, ln_g, ln_b,
              ffn1_w_gu, ffn1_w_down, w_in, w_out, lam_q1, lam_k1, lam_q2, lam_k2,
              diff_norm_g, rpb, ffn2_w_gu, ffn2_w_down):
    def run(x, c):
        for l in range(DEPTH):
            x = encoder_layer(x, c, l, w_ada[l], b_ada[l], ln_g[l], ln_b[l],
                              ffn1_w_gu[l], ffn1_w_down[l], w_in[l], w_out[l],
                              lam_q1[l], lam_k1[l], lam_q2[l], lam_k2[l],
                              diff_norm_g[l], rpb[l], ffn2_w_gu[l], ffn2_w_down[l])
        return x

    y_prompt = run(x_prompt, c_prompt)
    y_sample = run(x_sample, c_sample)
    return (y_prompt, y_sample)
```

```python
import functools
import math

import jax
import jax.numpy as jnp
from jax import lax
from jax.experimental import pallas as pl
from jax.experimental.pallas import tpu as pltpu

DIFF_HEADS = 8
DIFF_HEAD_DIM = 64
NA_HEADS = 8
NA_HEAD_DIM = 128
HEAD_COLS = 128
GROUP_COLS = 1024
NUM_GROUPS = 6
GRID_W = 64
NA_KH = 8
NA_KW = 16
ROPE_THETA = 10000.0
LN_EPS = 1e-5
RMS_EPS = 1e-5

NA_QROWS = 4
NA_WROWS = 12
NA_QTOK = NA_QROWS * GRID_W
NA_WTOK = NA_WROWS * GRID_W

VMEM_LIMIT = 56 * 1024 * 1024

F32 = jnp.float32
BF16 = jnp.bfloat16


def _params(*sem):
    return pltpu.CompilerParams(dimension_semantics=sem, vmem_limit_bytes=VMEM_LIMIT)


def _layer_norm(y, g, b):
    mu = jnp.mean(y, axis=-1, keepdims=True)
    d = y - mu
    var = jnp.mean(d * d, axis=-1, keepdims=True)
    return d * lax.rsqrt(var + LN_EPS) * g + b


def _ada_kernel(c_ref, w_ref, b_ref, o_ref):
    c = c_ref[...]
    a = c * jax.nn.sigmoid(c)
    o_ref[0] = jnp.dot(a, w_ref[0], preferred_element_type=F32) + b_ref[0]


def _ada(c_pad, w_ada, b_ada, tn=1024):
    depth, d, n = w_ada.shape
    rows = c_pad.shape[0]
    return pl.pallas_call(
        _ada_kernel,
        grid=(depth, n // tn),
        in_specs=[
            pl.BlockSpec((rows, d), lambda l, j: (0, 0)),
            pl.BlockSpec((1, d, tn), lambda l, j: (l, 0, j)),
            pl.BlockSpec((1, 1, tn), lambda l, j: (l, 0, j)),
        ],
        out_specs=pl.BlockSpec((1, rows, tn), lambda l, j: (l, 0, j)),
        out_shape=jax.ShapeDtypeStruct((depth, rows, n), F32),
        compiler_params=_params("arbitrary", "arbitrary"),
        name="ada",
    )(c_pad, w_ada, b_ada.reshape(depth, 1, n))


def _ffn_kernel(x_ref, mod_ref, wg_ref, wu_ref, wd_ref, lng_ref, lnb_ref, o_ref,
                h_ref, acc_ref, *, slot, alpha, nf):
    j = pl.program_id(2)

    @pl.when(j == 0)
    def _():
        shift = mod_ref[0, 3 * slot:3 * slot + 1, :]
        scale = mod_ref[0, 3 * slot + 1:3 * slot + 2, :]
        h_ref[...] = (x_ref[0] * (1 + scale) + shift).astype(BF16)

    h = h_ref[...]
    g = jnp.dot(h, wg_ref[...], preferred_element_type=F32)
    u = jnp.dot(h, wu_ref[...], preferred_element_type=F32)
    a = (g * jax.nn.sigmoid(g) * u).astype(BF16)
    contrib = jnp.dot(a, wd_ref[...], preferred_element_type=F32)

    @pl.when(j == 0)
    def _():
        acc_ref[...] = contrib

    @pl.when(j > 0)
    def _():
        acc_ref[...] += contrib

    @pl.when(j == nf - 1)
    def _():
        gate = mod_ref[0, 3 * slot + 2:3 * slot + 3, :]
        y = alpha * x_ref[0] + ((1 + gate) * 0.5) * acc_ref[...]
        o_ref[0] = _layer_norm(y, lng_ref[...], lnb_ref[...])


def _ffn(x, mod, w_gu, w_down, ln_g, ln_b, *, slot, alpha, tm=512, tf=512):
    b, t, d = x.shape
    dff = w_down.shape[0]
    nf = dff // tf
    kern = functools.partial(_ffn_kernel, slot=slot, alpha=alpha, nf=nf)
    return pl.pallas_call(
        kern,
        grid=(b, t // tm, nf),
        in_specs=[
            pl.BlockSpec((1, tm, d), lambda bb, i, j: (bb, i, 0)),
            pl.BlockSpec((1, 9, d), lambda bb, i, j: (bb, 0, 0)),
            pl.BlockSpec((d, tf), lambda bb, i, j: (0, j)),
            pl.BlockSpec((d, tf), lambda bb, i, j: (0, j + nf)),
            pl.BlockSpec((tf, d), lambda bb, i, j: (j, 0)),
            pl.BlockSpec((1, d), lambda bb, i, j: (0, 0)),
            pl.BlockSpec((1, d), lambda bb, i, j: (0, 0)),
        ],
        out_specs=pl.BlockSpec((1, tm, d), lambda bb, i, j: (bb, i, 0)),
        out_shape=jax.ShapeDtypeStruct((b, t, d), F32),
        scratch_shapes=[pltpu.VMEM((tm, d), BF16), pltpu.VMEM((tm, d), F32)],
        compiler_params=_params("parallel", "parallel", "arbitrary"),
        name="ffn",
    )(x, mod, w_gu, w_gu, w_down, ln_g.reshape(1, d), ln_b.reshape(1, d))


def _proj_kernel(x_ref, mod_ref, w_ref, cos_ref, sin_ref, o_ref, h_ref, *, slot, q_scale):
    j = pl.program_id(2)

    @pl.when(j == 0)
    def _():
        shift = mod_ref[0, 3 * slot:3 * slot + 1, :]
        scale = mod_ref[0, 3 * slot + 1:3 * slot + 2, :]
        h_ref[...] = (x_ref[0] * (1 + scale) + shift).astype(BF16)

    acc = jnp.dot(h_ref[...], w_ref[...], preferred_element_type=F32)

    @pl.when(j < 2)
    def _():
        cos = cos_ref[...]
        sin = sin_ref[...]
        lane = lax.broadcasted_iota(jnp.int32, cos.shape, 1)
        first_half = (lane & (DIFF_HEAD_DIM - 1)) < (DIFF_HEAD_DIM // 2)
        out_scale = jnp.where(j == 0, q_scale, 1.0).astype(F32)
        for hh in range(GROUP_COLS // HEAD_COLS):
            xs = acc[:, hh * HEAD_COLS:(hh + 1) * HEAD_COLS]
            fwd = pltpu.roll(xs, HEAD_COLS - DIFF_HEAD_DIM // 2, axis=1)
            bwd = pltpu.roll(xs, DIFF_HEAD_DIM // 2, axis=1)
            rot = jnp.where(first_half, fwd, bwd)
            val = (xs * cos + rot * sin) * out_scale
            o_ref[0, 0, :, hh * HEAD_COLS:(hh + 1) * HEAD_COLS] = val.astype(BF16)

    @pl.when(j >= 2)
    def _():
        o_ref[0, 0] = acc.astype(BF16)


def _proj(x, mod, w_in, cos_t, sin_t, *, slot, tm=512):
    b, t, d = x.shape
    kern = functools.partial(_proj_kernel, slot=slot, q_scale=DIFF_HEAD_DIM ** -0.5)
    return pl.pallas_call(
        kern,
        grid=(b, t // tm, NUM_GROUPS),
        in_specs=[
            pl.BlockSpec((1, tm, d), lambda bb, i, j: (bb, i, 0)),
            pl.BlockSpec((1, 9, d), lambda bb, i, j: (bb, 0, 0)),
            pl.BlockSpec((d, GROUP_COLS), lambda bb, i, j: (0, j)),
            pl.BlockSpec((tm, HEAD_COLS), lambda bb, i, j: (i, 0)),
            pl.BlockSpec((tm, HEAD_COLS), lambda bb, i, j: (i, 0)),
        ],
        out_specs=pl.BlockSpec((1, 1, tm, GROUP_COLS), lambda bb, i, j: (j, bb, i, 0)),
        out_shape=jax.ShapeDtypeStruct((NUM_GROUPS, b, t, GROUP_COLS), BF16),
        scratch_shapes=[pltpu.VMEM((tm, d), BF16)],
        compiler_params=_params("parallel", "parallel", "arbitrary"),
        name="proj",
    )(x, mod, w_in, cos_t, sin_t)


def _diff_kernel(q_ref, k_ref, v_ref, lq1_ref, lk1_ref, lq2_ref, lk2_ref, g_ref, o_ref,
                 qz_ref, *, tq, tk, nk, lam_init):
    q = q_ref[0, 0]
    lane = lax.broadcasted_iota(jnp.int32, q.shape, 1)
    zero = jnp.zeros_like(q)
    qz_ref[0:tq, :] = jnp.where(lane < DIFF_HEAD_DIM, q, zero)
    qz_ref[tq:2 * tq, :] = jnp.where(lane >= DIFF_HEAD_DIM, q, zero)

    def body(j, carry):
        m, l, acc = carry
        qz = qz_ref[...]
        start = pl.multiple_of(j * tk, tk)
        k = k_ref[0, 0, pl.ds(start, tk), :]
        v = v_ref[0, 0, pl.ds(start, tk), :]
        s = lax.dot_general(qz, k, (((1,), (1,)), ((), ())), preferred_element_type=F32)
        m_new = jnp.maximum(m, jnp.max(s, axis=1, keepdims=True))
        alpha = jnp.exp(m - m_new)
        p = jnp.exp(s - m_new)
        l = alpha * l + jnp.sum(p, axis=1, keepdims=True)
        acc = alpha * acc + jnp.dot(p.astype(BF16), v, preferred_element_type=F32)
        return m_new, l, acc

    m0 = jnp.full((2 * tq, 1), -jnp.inf, F32)
    l0 = jnp.zeros((2 * tq, 1), F32)
    a0 = jnp.zeros((2 * tq, HEAD_COLS), F32)
    _, l, acc = lax.fori_loop(0, nk, body, (m0, l0, a0))

    lam = (jnp.exp(jnp.sum(lq1_ref[...] * lk1_ref[...], axis=1, keepdims=True))
           - jnp.exp(jnp.sum(lq2_ref[...] * lk2_ref[...], axis=1, keepdims=True))
           + lam_init)
    o = acc / l
    a = o[0:tq] - lam * o[tq:2 * tq]
    a = a * lax.rsqrt(jnp.mean(a * a, axis=-1, keepdims=True) + RMS_EPS)
    a = a * g_ref[...] * (1.0 - lam_init)
    o_ref[0] = a.astype(BF16)


def _diff_attention(qkv, lq1, lk1, lq2, lk2, norm_g, *, lam_init, tq=512, tk=512):
    _, b, t, _ = qkv.shape
    kern = functools.partial(_diff_kernel, tq=tq, tk=tk, nk=t // tk, lam_init=lam_init)
    vec = lambda n: pl.BlockSpec((1, n), lambda bb, h, i: (0, 0))
    return pl.pallas_call(
        kern,
        grid=(b, DIFF_HEADS, t // tq),
        in_specs=[
            pl.BlockSpec((1, 1, tq, HEAD_COLS), lambda bb, h, i: (0, bb, i, h)),
            pl.BlockSpec((1, 1, t, HEAD_COLS), lambda bb, h, i: (1, bb, 0, h)),
            pl.BlockSpec((1, 1, t, HEAD_COLS), lambda bb, h, i: (2, bb, 0, h)),
            vec(DIFF_HEAD_DIM), vec(DIFF_HEAD_DIM), vec(DIFF_HEAD_DIM), vec(DIFF_HEAD_DIM),
            vec(HEAD_COLS),
        ],
        out_specs=pl.BlockSpec((1, tq, HEAD_COLS), lambda bb, h, i: (bb, i, h)),
        out_shape=jax.ShapeDtypeStruct((b, t, DIFF_HEADS * HEAD_COLS), BF16),
        scratch_shapes=[pltpu.VMEM((2 * tq, HEAD_COLS), BF16)],
        compiler_params=_params("parallel", "parallel", "arbitrary"),
        name="diff_attn",
    )(qkv, qkv, qkv, lq1.reshape(1, -1), lk1.reshape(1, -1), lq2.reshape(1, -1),
      lk2.reshape(1, -1), norm_g.reshape(1, -1))


def _na_bias_table(rpb):
    ql = jnp.arange(NA_QROWS)
    kl = jnp.arange(NA_WROWS)
    qc = jnp.arange(GRID_W)
    kc = jnp.arange(GRID_W)
    cs = jnp.clip(qc - NA_KW // 2, 0, GRID_W - NA_KW)
    col_valid = (kc[None, :] >= cs[:, None]) & (kc[None, :] < cs[:, None] + NA_KW)
    col_idx = jnp.clip(kc[None, :] - qc[:, None], -(NA_KW - 1), NA_KW - 1) + (NA_KW - 1)
    tables = []
    for win_lo, off0 in ((0, 0), (None, -NA_KH // 2), (NA_WROWS - NA_KH, -NA_KH)):
        off = kl[None, :] - ql[:, None] + off0
        if win_lo is None:
            row_valid = (off >= -(NA_KH // 2)) & (off < NA_KH // 2)
        else:
            row_valid = (kl[None, :] >= win_lo) & (kl[None, :] < win_lo + NA_KH)
            row_valid = jnp.broadcast_to(row_valid, off.shape)
        row_idx = jnp.clip(off + (NA_KH - 1), 0, 2 * NA_KH - 2)
        bias = rpb[:, row_idx[:, None, :, None], col_idx[None, :, None, :]]
        valid = row_valid[:, None, :, None] & col_valid[None, :, None, :]
        bias = jnp.where(valid[None], bias.astype(F32), -jnp.inf)
        tables.append(bias.reshape(rpb.shape[0], NA_QTOK, NA_WTOK))
    return jnp.stack(tables)


def _na_kernel(q_ref, k0_ref, k1_ref, k2_ref, v0_ref, v1_ref, v2_ref, bias_ref, o_ref, *, scale):
    k_refs = (k0_ref, k1_ref, k2_ref)
    v_refs = (v0_ref, v1_ref, v2_ref)
    for h in range(NA_HEADS):
        cols = slice(h * NA_HEAD_DIM, (h + 1) * NA_HEAD_DIM)
        q = q_ref[0, 0, :, cols]
        s = [lax.dot_general(q, kr[0, 0, :, cols], (((1,), (1,)), ((), ())),
                             preferred_element_type=F32) for kr in k_refs]
        s = [s[w] * scale + bias_ref[0, h, :, w * NA_QTOK:(w + 1) * NA_QTOK] for w in range(3)]
        m = jnp.maximum(jnp.maximum(jnp.max(s[0], axis=1, keepdims=True),
                                    jnp.max(s[1], axis=1, keepdims=True)),
                        jnp.max(s[2], axis=1, keepdims=True))
        p = [jnp.exp(sw - m) for sw in s]
        l = (jnp.sum(p[0], axis=1, keepdims=True) + jnp.sum(p[1], axis=1, keepdims=True)
             + jnp.sum(p[2], axis=1, keepdims=True))
        o = (jnp.dot(p[0].astype(BF16), v_refs[0][0, 0, :, cols], preferred_element_type=F32)
             + jnp.dot(p[1].astype(BF16), v_refs[1][0, 0, :, cols], preferred_element_type=F32)
             + jnp.dot(p[2].astype(BF16), v_refs[2][0, 0, :, cols], preferred_element_type=F32))
        o_ref[0, :, cols] = (o / l).astype(BF16)


def _na_attention(qkv, bias):
    _, b, t, _ = qkv.shape
    nblk = t // NA_QTOK
    assert nblk >= 3
    win = lambda i: jnp.clip(i - 1, 0, nblk - 3)
    pat = lambda i: jnp.where(i == 0, 0, jnp.where(i == nblk - 1, 2, 1))
    blk = (1, 1, NA_QTOK, GROUP_COLS)
    kv_spec = lambda g, w: pl.BlockSpec(blk, lambda bb, i: (g, bb, win(i) + w, 0))
    kern = functools.partial(_na_kernel, scale=NA_HEAD_DIM ** -0.5)
    return pl.pallas_call(
        kern,
        grid=(b, nblk),
        in_specs=[
            pl.BlockSpec(blk, lambda bb, i: (3, bb, i, 0)),
            kv_spec(4, 0), kv_spec(4, 1), kv_spec(4, 2),
            kv_spec(5, 0), kv_spec(5, 1), kv_spec(5, 2),
            pl.BlockSpec((1, NA_HEADS, NA_QTOK, NA_WTOK), lambda bb, i: (pat(i), 0, 0, 0)),
        ],
        out_specs=pl.BlockSpec((1, NA_QTOK, GROUP_COLS), lambda bb, i: (bb, i, 0)),
        out_shape=jax.ShapeDtypeStruct((b, t, GROUP_COLS), BF16),
        compiler_params=_params("parallel", "arbitrary"),
        name="na_attn",
    )(qkv, qkv, qkv, qkv, qkv, qkv, qkv, bias)


def _out_kernel(x_ref, od_ref, on_ref, mod_ref, w_ref, lng_ref, lnb_ref, o_ref, *, slot, alpha):
    half = od_ref.shape[-1]
    a = (jnp.dot(od_ref[0], w_ref[0:half, :], preferred_element_type=F32)
         + jnp.dot(on_ref[0], w_ref[half:2 * half, :], preferred_element_type=F32))
    gate = mod_ref[0, 3 * slot + 2:3 * slot + 3, :]
    y = alpha * x_ref[0] + (1 + gate) * a
    o_ref[0] = _layer_norm(y, lng_ref[...], lnb_ref[...])


def _out_proj(x, o_diff, o_na, mod, w_out, ln_g, ln_b, *, slot, alpha, tm=512):
    b, t, d = x.shape
    half = o_diff.shape[-1]
    kern = functools.partial(_out_kernel, slot=slot, alpha=alpha)
    return pl.pallas_call(
        kern,
        grid=(b, t // tm),
        in_specs=[
            pl.BlockSpec((1, tm, d), lambda bb, i: (bb, i, 0)),
            pl.BlockSpec((1, tm, half), lambda bb, i: (bb, i, 0)),
            pl.BlockSpec((1, tm, half), lambda bb, i: (bb, i, 0)),
            pl.BlockSpec((1, 9, d), lambda bb, i: (bb, 0, 0)),
            pl.BlockSpec((2 * half, d), lambda bb, i: (0, 0)),
            pl.BlockSpec((1, d), lambda bb, i: (0, 0)),
            pl.BlockSpec((1, d), lambda bb, i: (0, 0)),
        ],
        out_specs=pl.BlockSpec((1, tm, d), lambda bb, i: (bb, i, 0)),
        out_shape=jax.ShapeDtypeStruct((b, t, d), F32),
        compiler_params=_params("parallel", "parallel"),
        name="out_proj",
    )(x, o_diff, o_na, mod, w_out, ln_g.reshape(1, d), ln_b.reshape(1, d))


def _rotary_tables(t):
    half = DIFF_HEAD_DIM // 2
    inv_freq = ROPE_THETA ** (-jnp.arange(0, DIFF_HEAD_DIM, 2, dtype=F32) / DIFF_HEAD_DIM)
    ang = jnp.arange(t, dtype=F32)[:, None] * inv_freq[None, :]
    cos = jnp.cos(ang)
    sin = jnp.sin(ang)
    reps = HEAD_COLS // half
    cos_t = jnp.concatenate([cos] * reps, axis=-1)
    sin_t = jnp.concatenate([-sin, sin] * (reps // 2), axis=-1)
    return cos_t, sin_t


def kernel(x_prompt, x_sample, c_prompt, c_sample, w_ada, b_ada, ln_g, ln_b, ffn1_w_gu, ffn1_w_down, w_in, w_out, lam_q1, lam_k1, lam_q2, lam_k2, diff_norm_g, rpb, ffn2_w_gu, ffn2_w_down):
    depth, d, _ = w_ada.shape
    alpha = (2.0 * depth) ** 0.25

    groups = ((x_prompt, c_prompt), (x_sample, c_sample))
    n_cond = sum(c.shape[0] for _, c in groups)
    rows = -(-n_cond // 8) * 8
    c_all = jnp.concatenate([c for _, c in groups] + [jnp.zeros((rows - n_cond, d), F32)], axis=0)
    mod_all = _ada(c_all, w_ada, b_ada)

    w1gu = ffn1_w_gu.astype(BF16)
    w1d = ffn1_w_down.astype(BF16)
    w2gu = ffn2_w_gu.astype(BF16)
    w2d = ffn2_w_down.astype(BF16)
    win = w_in.astype(BF16)
    wout = w_out.astype(BF16)

    outs = []
    row0 = 0
    for x, c in groups:
        b, t, _ = x.shape
        cos_t, sin_t = _rotary_tables(t)
        for l in range(depth):
            mod = mod_all[l, row0:row0 + b].reshape(b, 9, d)
            lam_init = 0.8 - 0.6 * math.exp(-0.3 * l)
            x = _ffn(x, mod, w1gu[l], w1d[l], ln_g[l, 0], ln_b[l, 0], slot=0, alpha=alpha)
            qkv = _proj(x, mod, win[l], cos_t, sin_t, slot=1)
            o_diff = _diff_attention(qkv, lam_q1[l], lam_k1[l], lam_q2[l], lam_k2[l],
                                     diff_norm_g[l], lam_init=lam_init)
            o_na = _na_attention(qkv, _na_bias_table(rpb[l]))
            x = _out_proj(x, o_diff, o_na, mod, wout[l], ln_g[l, 1], ln_b[l, 1],
                          slot=1, alpha=alpha)
            x = _ffn(x, mod, w2gu[l], w2d[l], ln_g[l, 2], ln_b[l, 2], slot=2, alpha=alpha)
        outs.append(x)
        row0 += b
    return tuple(outs)
```

```python
import functools
import math

import jax
import jax.numpy as jnp
from jax import lax
from jax.experimental import pallas as pl
from jax.experimental.pallas import tpu as pltpu

DIFF_HEADS = 8
DIFF_HEAD_DIM = 64
NA_HEADS = 8
NA_HEAD_DIM = 128
HEAD_COLS = 128
GROUP_COLS = 1024
NUM_GROUPS = 6
GRID_W = 64
NA_KH = 8
NA_KW = 16
ROPE_THETA = 10000.0
LN_EPS = 1e-5
RMS_EPS = 1e-5

DIFF_TQ = 512
DIFF_TK = 512
DIFF_CW = 256

NA_QROWS = 4
NA_WROWS = 12
NA_QTOK = NA_QROWS * GRID_W
NA_WTOK = NA_WROWS * GRID_W

VMEM_LIMIT = 56 * 1024 * 1024

F32 = jnp.float32
BF16 = jnp.bfloat16
LOG2E = math.log2(math.e)


def _params(*sem):
    return pltpu.CompilerParams(dimension_semantics=sem, vmem_limit_bytes=VMEM_LIMIT)


def _layer_norm(y, g, b):
    mu = jnp.mean(y, axis=-1, keepdims=True)
    d = y - mu
    var = jnp.mean(d * d, axis=-1, keepdims=True)
    return d * lax.rsqrt(var + LN_EPS) * g + b


def _ada_kernel(c_ref, w_ref, b_ref, o_ref):
    c = c_ref[...]
    a = c * jax.nn.sigmoid(c)
    o_ref[0] = jnp.dot(a, w_ref[0], preferred_element_type=F32) + b_ref[0]


def _ada(c_pad, w_ada, b_ada, tn=1024):
    depth, d, n = w_ada.shape
    rows = c_pad.shape[0]
    return pl.pallas_call(
        _ada_kernel,
        grid=(depth, n // tn),
        in_specs=[
            pl.BlockSpec((rows, d), lambda l, j: (0, 0)),
            pl.BlockSpec((1, d, tn), lambda l, j: (l, 0, j)),
            pl.BlockSpec((1, 1, tn), lambda l, j: (l, 0, j)),
        ],
        out_specs=pl.BlockSpec((1, rows, tn), lambda l, j: (l, 0, j)),
        out_shape=jax.ShapeDtypeStruct((depth, rows, n), F32),
        compiler_params=_params("arbitrary", "arbitrary"),
        name="ada",
    )(c_pad, w_ada, b_ada.reshape(depth, 1, n))


def _ffn_kernel(x_ref, mod_ref, wg_ref, wu_ref, wd_ref, lng_ref, lnb_ref, o_ref,
                h_ref, acc_ref, *, slot, alpha, nf):
    j = pl.program_id(2)

    @pl.when(j == 0)
    def _():
        shift = mod_ref[0, 3 * slot:3 * slot + 1, :]
        scale = mod_ref[0, 3 * slot + 1:3 * slot + 2, :]
        h_ref[...] = (x_ref[0] * (1 + scale) + shift).astype(BF16)

    h = h_ref[...]
    g = jnp.dot(h, wg_ref[...], preferred_element_type=F32)
    u = jnp.dot(h, wu_ref[...], preferred_element_type=F32)
    a = (g * jax.nn.sigmoid(g) * u).astype(BF16)
    contrib = jnp.dot(a, wd_ref[...], preferred_element_type=F32)

    @pl.when(j == 0)
    def _():
        acc_ref[...] = contrib

    @pl.when(j > 0)
    def _():
        acc_ref[...] += contrib

    @pl.when(j == nf - 1)
    def _():
        gate = mod_ref[0, 3 * slot + 2:3 * slot + 3, :]
        y = alpha * x_ref[0] + ((1 + gate) * 0.5) * acc_ref[...]
        o_ref[0] = _layer_norm(y, lng_ref[...], lnb_ref[...])


def _ffn(x, mod, w_gu, w_down, ln_g, ln_b, *, slot, alpha, tm=512, tf=512):
    b, t, d = x.shape
    dff = w_down.shape[0]
    nf = dff // tf
    kern = functools.partial(_ffn_kernel, slot=slot, alpha=alpha, nf=nf)
    return pl.pallas_call(
        kern,
        grid=(b, t // tm, nf),
        in_specs=[
            pl.BlockSpec((1, tm, d), lambda bb, i, j: (bb, i, 0)),
            pl.BlockSpec((1, 9, d), lambda bb, i, j: (bb, 0, 0)),
            pl.BlockSpec((d, tf), lambda bb, i, j: (0, j)),
            pl.BlockSpec((d, tf), lambda bb, i, j: (0, j + nf)),
            pl.BlockSpec((tf, d), lambda bb, i, j: (j, 0)),
            pl.BlockSpec((1, d), lambda bb, i, j: (0, 0)),
            pl.BlockSpec((1, d), lambda bb, i, j: (0, 0)),
        ],
        out_specs=pl.BlockSpec((1, tm, d), lambda bb, i, j: (bb, i, 0)),
        out_shape=jax.ShapeDtypeStruct((b, t, d), F32),
        scratch_shapes=[pltpu.VMEM((tm, d), BF16), pltpu.VMEM((tm, d), F32)],
        compiler_params=_params("parallel", "parallel", "arbitrary"),
        name="ffn",
    )(x, mod, w_gu, w_gu, w_down, ln_g.reshape(1, d), ln_b.reshape(1, d))


def _proj_kernel(x_ref, mod_ref, w_ref, cos_ref, sin_ref,
                 qt_ref, k_ref, vt_ref, nq_ref, nk_ref, nv_ref, *, slot, q_scale, tk):
    shift = mod_ref[0, 3 * slot:3 * slot + 1, :]
    scale = mod_ref[0, 3 * slot + 1:3 * slot + 2, :]
    h = (x_ref[0] * (1 + scale) + shift).astype(BF16)
    tm = h.shape[0]

    def group(g):
        return jnp.dot(h, w_ref[:, g * GROUP_COLS:(g + 1) * GROUP_COLS],
                       preferred_element_type=F32)

    cos = cos_ref[...]
    sin = sin_ref[...]
    lane = lax.broadcasted_iota(jnp.int32, cos.shape, 1)
    first_half = (lane & (DIFF_HEAD_DIM - 1)) < (DIFF_HEAD_DIM // 2)

    def rotary(xs):
        fwd = pltpu.roll(xs, HEAD_COLS - DIFF_HEAD_DIM // 2, axis=1)
        bwd = pltpu.roll(xs, DIFF_HEAD_DIM // 2, axis=1)
        return xs * cos + jnp.where(first_half, fwd, bwd) * sin

    acc = group(0)
    for hh in range(DIFF_HEADS):
        cols = slice(hh * HEAD_COLS, (hh + 1) * HEAD_COLS)
        qt_ref[0, 0, cols, :] = (rotary(acc[:, cols]) * q_scale).T.astype(BF16)
    acc = group(1)
    for hh in range(DIFF_HEADS):
        cols = slice(hh * HEAD_COLS, (hh + 1) * HEAD_COLS)
        k_ref[0, :, cols] = rotary(acc[:, cols]).astype(BF16)
    acc = group(2)
    for cc in range(tm // tk):
        vt_ref[0, cc] = acc[cc * tk:(cc + 1) * tk, :].T.astype(BF16)
    nq_ref[0] = group(3).astype(BF16)
    nk_ref[0] = group(4).astype(BF16)
    nv_ref[0] = group(5).astype(BF16)


def _proj(x, mod, w_in, cos_t, sin_t, *, slot):
    b, t, d = x.shape
    tm, tk = DIFF_TQ, DIFF_TK
    kern = functools.partial(_proj_kernel, slot=slot, q_scale=DIFF_HEAD_DIM ** -0.5 * LOG2E, tk=tk)
    row_spec = pl.BlockSpec((1, tm, GROUP_COLS), lambda bb, i: (bb, i, 0))
    row_shape = jax.ShapeDtypeStruct((b, t, GROUP_COLS), BF16)
    return pl.pallas_call(
        kern,
        grid=(b, t // tm),
        in_specs=[
            pl.BlockSpec((1, tm, d), lambda bb, i: (bb, i, 0)),
            pl.BlockSpec((1, 9, d), lambda bb, i: (bb, 0, 0)),
            pl.BlockSpec((d, NUM_GROUPS * GROUP_COLS), lambda bb, i: (0, 0),
                         pipeline_mode=pl.Buffered(1)),
            pl.BlockSpec((tm, HEAD_COLS), lambda bb, i: (i, 0)),
            pl.BlockSpec((tm, HEAD_COLS), lambda bb, i: (i, 0)),
        ],
        out_specs=[
            pl.BlockSpec((1, 1, GROUP_COLS, tm), lambda bb, i: (bb, i, 0, 0)),
            row_spec,
            pl.BlockSpec((1, tm // tk, GROUP_COLS, tk), lambda bb, i: (bb, i, 0, 0)),
            row_spec, row_spec, row_spec,
        ],
        out_shape=[
            jax.ShapeDtypeStruct((b, t // tm, GROUP_COLS, tm), BF16),
            row_shape,
            jax.ShapeDtypeStruct((b, t // tk, GROUP_COLS, tk), BF16),
            row_shape, row_shape, row_shape,
        ],
        compiler_params=_params("parallel", "parallel"),
        name="proj",
    )(x, mod, w_in, cos_t, sin_t)


def _diff_kernel(qt_ref, k_ref, vt_ref, lq1_ref, lk1_ref, lq2_ref, lk2_ref, g_ref, o_ref,
                 qz_ref, st_ref, m_ref, l_ref, acc_ref, *, tq, tk, cw, nk, lam_init):
    qt = qt_ref[0, 0]
    row = lax.broadcasted_iota(jnp.int32, qt.shape, 0)
    zero = jnp.zeros_like(qt)
    qz_ref[:, 0:tq] = jnp.where(row < DIFF_HEAD_DIM, qt, zero)
    qz_ref[:, tq:2 * tq] = jnp.where(row >= DIFF_HEAD_DIM, qt, zero)
    m_ref[...] = jnp.full(m_ref.shape, -jnp.inf, F32)
    l_ref[...] = jnp.zeros(l_ref.shape, F32)
    acc_ref[...] = jnp.zeros(acc_ref.shape, F32)

    def scores(j, slot):
        start = pl.multiple_of(j * tk, tk)
        k = k_ref[0, pl.ds(start, tk), :]
        st_ref[slot] = jnp.dot(k, qz_ref[...], preferred_element_type=F32)

    def accumulate(j, slot):
        vt = vt_ref[0, j]
        for c in range(2 * tq // cw):
            cols = slice(c * cw, (c + 1) * cw)
            st = st_ref[slot, :, cols]
            m_old = m_ref[:, cols]
            m_new = jnp.maximum(m_old, jnp.max(st, axis=0, keepdims=True))
            alpha = jnp.exp2(m_old - m_new)
            p = jnp.exp2(st - m_new)
            l_ref[:, cols] = alpha * l_ref[:, cols] + jnp.sum(p, axis=0, keepdims=True)
            acc_ref[:, cols] = alpha * acc_ref[:, cols] + jnp.dot(
                vt, p.astype(BF16), preferred_element_type=F32)
            m_ref[:, cols] = m_new

    def body(jj, carry):
        j = 2 * jj
        scores(j + 1, 1)
        accumulate(j, 0)
        scores(j + 2, 0)
        accumulate(j + 1, 1)
        return carry

    scores(0, 0)
    lax.fori_loop(0, nk // 2 - 1, body, 0)
    scores(nk - 1, 1)
    accumulate(nk - 2, 0)
    accumulate(nk - 1, 1)

    lam = (jnp.exp(jnp.sum(lq1_ref[...] * lk1_ref[...], axis=1, keepdims=True))
           - jnp.exp(jnp.sum(lq2_ref[...] * lk2_ref[...], axis=1, keepdims=True))
           + lam_init)
    ot = acc_ref[...] / l_ref[...]
    a = (ot[:, 0:tq] - lam * ot[:, tq:2 * tq]).T
    a = a * lax.rsqrt(jnp.mean(a * a, axis=-1, keepdims=True) + RMS_EPS)
    a = a * g_ref[...] * (1.0 - lam_init)
    o_ref[0] = a.astype(BF16)


def _diff_attention(qt, k, vt, lq1, lk1, lq2, lk2, norm_g, *, lam_init):
    b, t, _ = k.shape
    tq, tk, cw = DIFF_TQ, DIFF_TK, DIFF_CW
    kern = functools.partial(_diff_kernel, tq=tq, tk=tk, cw=cw, nk=t // tk, lam_init=lam_init)
    vec = lambda n: pl.BlockSpec((1, n), lambda bb, h, i: (0, 0))
    return pl.pallas_call(
        kern,
        grid=(b, DIFF_HEADS, t // tq),
        in_specs=[
            pl.BlockSpec((1, 1, HEAD_COLS, tq), lambda bb, h, i: (bb, i, h, 0)),
            pl.BlockSpec((1, t, HEAD_COLS), lambda bb, h, i: (bb, 0, h)),
            pl.BlockSpec((1, t // tk, HEAD_COLS, tk), lambda bb, h, i: (bb, 0, h, 0)),
            vec(DIFF_HEAD_DIM), vec(DIFF_HEAD_DIM), vec(DIFF_HEAD_DIM), vec(DIFF_HEAD_DIM),
            vec(HEAD_COLS),
        ],
        out_specs=pl.BlockSpec((1, tq, HEAD_COLS), lambda bb, h, i: (bb, i, h)),
        out_shape=jax.ShapeDtypeStruct((b, t, DIFF_HEADS * HEAD_COLS), BF16),
        scratch_shapes=[
            pltpu.VMEM((HEAD_COLS, 2 * tq), BF16),
            pltpu.VMEM((2, tk, 2 * tq), F32),
            pltpu.VMEM((1, 2 * tq), F32),
            pltpu.VMEM((1, 2 * tq), F32),
            pltpu.VMEM((HEAD_COLS, 2 * tq), F32),
        ],
        compiler_params=_params("parallel", "parallel", "arbitrary"),
        name="diff_attn",
    )(qt, k, vt, lq1.reshape(1, -1), lk1.reshape(1, -1), lq2.reshape(1, -1),
      lk2.reshape(1, -1), norm_g.reshape(1, -1))


def _na_bias_table(rpb):
    heads = rpb.shape[0]
    qc = jnp.arange(GRID_W)
    kc = jnp.arange(GRID_W)
    cs = jnp.clip(qc - NA_KW // 2, 0, GRID_W - NA_KW)
    col_valid = (kc[None, :] >= cs[:, None]) & (kc[None, :] < cs[:, None] + NA_KW)
    delta = kc[None, :] - qc[:, None] + (NA_KW - 1)
    sel = delta[:, :, None] == jnp.arange(2 * NA_KW - 1)
    col_exp = jnp.sum(jnp.where(sel[None, None], rpb.astype(F32)[:, :, None, None, :], 0.0), axis=-1)
    col_exp = jnp.where(col_valid[None, None], col_exp, -jnp.inf)
    masked = jnp.full((heads, GRID_W, GRID_W), -jnp.inf, F32)
    tables = []
    for win_lo, off0 in ((0, 0), (None, -(NA_KH // 2)), (NA_WROWS - NA_KH, -NA_KH)):
        q_rows = []
        for ql in range(NA_QROWS):
            tiles = []
            for kl in range(NA_WROWS):
                off = kl - ql + off0
                if win_lo is None:
                    valid = -(NA_KH // 2) <= off < NA_KH // 2
                else:
                    valid = win_lo <= kl < win_lo + NA_KH
                tiles.append(col_exp[:, off + NA_KH - 1] if valid else masked)
            q_rows.append(jnp.stack(tiles, axis=2))
        tables.append(jnp.stack(q_rows, axis=1).reshape(heads, NA_QTOK, NA_WTOK))
    return jnp.stack(tables)


def _na_kernel(q_ref, k0_ref, k1_ref, k2_ref, v0_ref, v1_ref, v2_ref, bias_ref, o_ref, *, scale):
    k_refs = (k0_ref, k1_ref, k2_ref)
    v_refs = (v0_ref, v1_ref, v2_ref)
    wb = NA_WTOK // 3
    for h in range(NA_HEADS):
        cols = slice(h * NA_HEAD_DIM, (h + 1) * NA_HEAD_DIM)
        q = q_ref[0, :, cols]
        s = [lax.dot_general(q, kr[0, :, cols], (((1,), (1,)), ((), ())),
                             preferred_element_type=F32) for kr in k_refs]
        s = [s[w] * scale + bias_ref[0, h, :, w * wb:(w + 1) * wb] for w in range(3)]
        m = jnp.maximum(jnp.maximum(jnp.max(s[0], axis=1, keepdims=True),
                                    jnp.max(s[1], axis=1, keepdims=True)),
                        jnp.max(s[2], axis=1, keepdims=True))
        p = [jnp.exp(sw - m) for sw in s]
        l = (jnp.sum(p[0], axis=1, keepdims=True) + jnp.sum(p[1], axis=1, keepdims=True)
             + jnp.sum(p[2], axis=1, keepdims=True))
        o = (jnp.dot(p[0].astype(BF16), v_refs[0][0, :, cols], preferred_element_type=F32)
             + jnp.dot(p[1].astype(BF16), v_refs[1][0, :, cols], preferred_element_type=F32)
             + jnp.dot(p[2].astype(BF16), v_refs[2][0, :, cols], preferred_element_type=F32))
        o_ref[0, :, cols] = (o / l).astype(BF16)


def _na_attention(nq, nk, nv, bias):
    b, t, _ = nq.shape
    nblk = t // NA_QTOK
    assert nblk >= 3
    win = lambda i: jnp.clip(i - 1, 0, nblk - 3)
    pat = lambda i: jnp.where(i == 0, 0, jnp.where(i == nblk - 1, 2, 1))
    blk = (1, NA_QTOK, GROUP_COLS)
    kv_spec = lambda w: pl.BlockSpec(blk, lambda bb, i: (bb, win(i) + w, 0))
    kern = functools.partial(_na_kernel, scale=NA_HEAD_DIM ** -0.5)
    return pl.pallas_call(
        kern,
        grid=(b, nblk),
        in_specs=[
            pl.BlockSpec(blk, lambda bb, i: (bb, i, 0)),
            kv_spec(0), kv_spec(1), kv_spec(2),
            kv_spec(0), kv_spec(1), kv_spec(2),
            pl.BlockSpec((1, NA_HEADS, NA_QTOK, NA_WTOK), lambda bb, i: (pat(i), 0, 0, 0)),
        ],
        out_specs=pl.BlockSpec(blk, lambda bb, i: (bb, i, 0)),
        out_shape=jax.ShapeDtypeStruct((b, t, GROUP_COLS), BF16),
        compiler_params=_params("parallel", "arbitrary"),
        name="na_attn",
    )(nq, nk, nk, nk, nv, nv, nv, bias)


def _out_kernel(x_ref, od_ref, on_ref, mod_ref, w_ref, lng_ref, lnb_ref, o_ref, *, slot, alpha):
    half = od_ref.shape[-1]
    a = (jnp.dot(od_ref[0], w_ref[0:half, :], preferred_element_type=F32)
         + jnp.dot(on_ref[0], w_ref[half:2 * half, :], preferred_element_type=F32))
    gate = mod_ref[0, 3 * slot + 2:3 * slot + 3, :]
    y = alpha * x_ref[0] + (1 + gate) * a
    o_ref[0] = _layer_norm(y, lng_ref[...], lnb_ref[...])


def _out_proj(x, o_diff, o_na, mod, w_out, ln_g, ln_b, *, slot, alpha, tm=512):
    b, t, d = x.shape
    half = o_diff.shape[-1]
    kern = functools.partial(_out_kernel, slot=slot, alpha=alpha)
    return pl.pallas_call(
        kern,
        grid=(b, t // tm),
        in_specs=[
            pl.BlockSpec((1, tm, d), lambda bb, i: (bb, i, 0)),
            pl.BlockSpec((1, tm, half), lambda bb, i: (bb, i, 0)),
            pl.BlockSpec((1, tm, half), lambda bb, i: (bb, i, 0)),
            pl.BlockSpec((1, 9, d), lambda bb, i: (bb, 0, 0)),
            pl.BlockSpec((2 * half, d), lambda bb, i: (0, 0)),
            pl.BlockSpec((1, d), lambda bb, i: (0, 0)),
            pl.BlockSpec((1, d), lambda bb, i: (0, 0)),
        ],
        out_specs=pl.BlockSpec((1, tm, d), lambda bb, i: (bb, i, 0)),
        out_shape=jax.ShapeDtypeStruct((b, t, d), F32),
        compiler_params=_params("parallel", "parallel"),
        name="out_proj",
    )(x, o_diff, o_na, mod, w_out, ln_g.reshape(1, d), ln_b.reshape(1, d))


def _rotary_tables(t):
    half = DIFF_HEAD_DIM // 2
    inv_freq = ROPE_THETA ** (-jnp.arange(0, DIFF_HEAD_DIM, 2, dtype=F32) / DIFF_HEAD_DIM)
    ang = jnp.arange(t, dtype=F32)[:, None] * inv_freq[None, :]
    cos = jnp.cos(ang)
    sin = jnp.sin(ang)
    reps = HEAD_COLS // half
    cos_t = jnp.concatenate([cos] * reps, axis=-1)
    sin_t = jnp.concatenate([-sin, sin] * (reps // 2), axis=-1)
    return cos_t, sin_t


def kernel(x_prompt, x_sample, c_prompt, c_sample, w_ada, b_ada, ln_g, ln_b, ffn1_w_gu, ffn1_w_down, w_in, w_out, lam_q1, lam_k1, lam_q2, lam_k2, diff_norm_g, rpb, ffn2_w_gu, ffn2_w_down):
    depth, d, _ = w_ada.shape
    alpha = (2.0 * depth) ** 0.25

    groups = ((x_prompt, c_prompt), (x_sample, c_sample))
    n_cond = sum(c.shape[0] for _, c in groups)
    rows = -(-n_cond // 8) * 8
    c_all = jnp.concatenate([c for _, c in groups] + [jnp.zeros((rows - n_cond, d), F32)], axis=0)
    mod_all = _ada(c_all, w_ada, b_ada)

    w1gu = ffn1_w_gu.astype(BF16)
    w1d = ffn1_w_down.astype(BF16)
    w2gu = ffn2_w_gu.astype(BF16)
    w2d = ffn2_w_down.astype(BF16)
    win = w_in.astype(BF16)
    wout = w_out.astype(BF16)
    na_bias = [_na_bias_table(rpb[l]) for l in range(depth)]

    outs = []
    row0 = 0
    for x, c in groups:
        b, t, _ = x.shape
        cos_t, sin_t = _rotary_tables(t)
        for l in range(depth):
            mod = mod_all[l, row0:row0 + b].reshape(b, 9, d)
            lam_init = 0.8 - 0.6 * math.exp(-0.3 * l)
            x = _ffn(x, mod, w1gu[l], w1d[l], ln_g[l, 0], ln_b[l, 0], slot=0, alpha=alpha)
            qt, k, vt, nq, nk, nv = _proj(x, mod, win[l], cos_t, sin_t, slot=1)
            o_diff = _diff_attention(qt, k, vt, lam_q1[l], lam_k1[l], lam_q2[l], lam_k2[l],
                                     diff_norm_g[l], lam_init=lam_init)
            o_na = _na_attention(nq, nk, nv, na_bias[l])
            x = _out_proj(x, o_diff, o_na, mod, wout[l], ln_g[l, 1], ln_b[l, 1],
                          slot=1, alpha=alpha)
            x = _ffn(x, mod, w2gu[l], w2d[l], ln_g[l, 2], ln_b[l, 2], slot=2, alpha=alpha)
        outs.append(x)
        row0 += b
    return tuple(outs)
```

```python
import functools
import math

import jax
import jax.numpy as jnp
from jax import lax
from jax.experimental import pallas as pl
from jax.experimental.pallas import tpu as pltpu

DIFF_HEADS = 8
DIFF_HEAD_DIM = 64
NA_HEADS = 8
NA_HEAD_DIM = 128
HEAD_COLS = 128
GROUP_COLS = 1024
NUM_GROUPS = 6
GRID_W = 64
NA_KH = 8
NA_KW = 16
ROPE_THETA = 10000.0
LN_EPS = 1e-5
RMS_EPS = 1e-5

DIFF_TQ = 512
DIFF_TK = 512
DIFF_CW = 256
DIFF_STEPS_PER_TRIP = 8

NA_QROWS = 4
NA_WROWS = 12
NA_QTOK = NA_QROWS * GRID_W
NA_WTOK = NA_WROWS * GRID_W

VMEM_LIMIT = 56 * 1024 * 1024

F32 = jnp.float32
BF16 = jnp.bfloat16
LOG2E = math.log2(math.e)


def _params(*sem):
    return pltpu.CompilerParams(dimension_semantics=sem, vmem_limit_bytes=VMEM_LIMIT)


def _layer_norm(y, g, b):
    mu = jnp.mean(y, axis=-1, keepdims=True)
    d = y - mu
    var = jnp.mean(d * d, axis=-1, keepdims=True)
    return d * lax.rsqrt(var + LN_EPS) * g + b


def _ada_kernel(c_ref, w_ref, b_ref, o_ref):
    c = c_ref[...]
    a = c * jax.nn.sigmoid(c)
    o_ref[0] = jnp.dot(a, w_ref[0], preferred_element_type=F32) + b_ref[0]


def _ada(c_pad, w_ada, b_ada, tn=1024):
    depth, d, n = w_ada.shape
    rows = c_pad.shape[0]
    return pl.pallas_call(
        _ada_kernel,
        grid=(depth, n // tn),
        in_specs=[
            pl.BlockSpec((rows, d), lambda l, j: (0, 0)),
            pl.BlockSpec((1, d, tn), lambda l, j: (l, 0, j)),
            pl.BlockSpec((1, 1, tn), lambda l, j: (l, 0, j)),
        ],
        out_specs=pl.BlockSpec((1, rows, tn), lambda l, j: (l, 0, j)),
        out_shape=jax.ShapeDtypeStruct((depth, rows, n), F32),
        compiler_params=_params("arbitrary", "arbitrary"),
        name="ada",
    )(c_pad, w_ada, b_ada.reshape(depth, 1, n))


def _ffn_kernel(x_ref, mod_ref, wg_ref, wu_ref, wd_ref, lng_ref, lnb_ref, o_ref,
                h_ref, acc_ref, *, slot, alpha, nf):
    j = pl.program_id(2)

    @pl.when(j == 0)
    def _():
        shift = mod_ref[0, 3 * slot:3 * slot + 1, :]
        scale = mod_ref[0, 3 * slot + 1:3 * slot + 2, :]
        h_ref[...] = (x_ref[0] * (1 + scale) + shift).astype(BF16)
        acc_ref[...] = jnp.zeros(acc_ref.shape, F32)

    h = h_ref[...]
    g = jnp.dot(h, wg_ref[...], preferred_element_type=F32)
    u = jnp.dot(h, wu_ref[...], preferred_element_type=F32)
    a = (g * jax.nn.sigmoid(g) * u).astype(BF16)
    acc_ref[...] += jnp.dot(a, wd_ref[...], preferred_element_type=F32)

    @pl.when(j == nf - 1)
    def _():
        gate = mod_ref[0, 3 * slot + 2:3 * slot + 3, :]
        y = alpha * x_ref[0] + ((1 + gate) * 0.5) * acc_ref[...]
        o_ref[0] = _layer_norm(y, lng_ref[...], lnb_ref[...])


def _ffn(x, mod, w_gu, w_down, ln_g, ln_b, *, slot, alpha, tm=512, tf=512):
    b, t, d = x.shape
    dff = w_down.shape[0]
    nf = dff // tf
    kern = functools.partial(_ffn_kernel, slot=slot, alpha=alpha, nf=nf)
    return pl.pallas_call(
        kern,
        grid=(b, t // tm, nf),
        in_specs=[
            pl.BlockSpec((1, tm, d), lambda bb, i, j: (bb, i, 0)),
            pl.BlockSpec((1, 9, d), lambda bb, i, j: (bb, 0, 0)),
            pl.BlockSpec((d, tf), lambda bb, i, j: (0, j)),
            pl.BlockSpec((d, tf), lambda bb, i, j: (0, j + nf)),
            pl.BlockSpec((tf, d), lambda bb, i, j: (j, 0)),
            pl.BlockSpec((1, d), lambda bb, i, j: (0, 0)),
            pl.BlockSpec((1, d), lambda bb, i, j: (0, 0)),
        ],
        out_specs=pl.BlockSpec((1, tm, d), lambda bb, i, j: (bb, i, 0)),
        out_shape=jax.ShapeDtypeStruct((b, t, d), F32),
        scratch_shapes=[pltpu.VMEM((tm, d), BF16), pltpu.VMEM((tm, d), F32)],
        compiler_params=_params("parallel", "parallel", "arbitrary"),
        name="ffn",
    )(x, mod, w_gu, w_gu, w_down, ln_g.reshape(1, d), ln_b.reshape(1, d))


def _proj_kernel(x_ref, mod_ref, w_ref, cos_ref, sin_ref,
                 qt_ref, k_ref, vt_ref, nq_ref, nk_ref, nv_ref, *, slot, q_scale, tk):
    shift = mod_ref[0, 3 * slot:3 * slot + 1, :]
    scale = mod_ref[0, 3 * slot + 1:3 * slot + 2, :]
    h = (x_ref[0] * (1 + scale) + shift).astype(BF16)
    tm = h.shape[0]

    def group(g):
        return jnp.dot(h, w_ref[:, g * GROUP_COLS:(g + 1) * GROUP_COLS],
                       preferred_element_type=F32)

    cos = cos_ref[...]
    sin = sin_ref[...]
    lane = lax.broadcasted_iota(jnp.int32, cos.shape, 1)
    first_half = (lane & (DIFF_HEAD_DIM - 1)) < (DIFF_HEAD_DIM // 2)

    def rotary(xs):
        fwd = pltpu.roll(xs, HEAD_COLS - DIFF_HEAD_DIM // 2, axis=1)
        bwd = pltpu.roll(xs, DIFF_HEAD_DIM // 2, axis=1)
        return xs * cos + jnp.where(first_half, fwd, bwd) * sin

    acc = group(0)
    for hh in range(DIFF_HEADS):
        cols = slice(hh * HEAD_COLS, (hh + 1) * HEAD_COLS)
        qt_ref[0, 0, cols, :] = (rotary(acc[:, cols]) * q_scale).T.astype(BF16)
    acc = group(1)
    for hh in range(DIFF_HEADS):
        cols = slice(hh * HEAD_COLS, (hh + 1) * HEAD_COLS)
        k_ref[0, :, cols] = rotary(acc[:, cols]).astype(BF16)
    acc = group(2)
    for cc in range(tm // tk):
        vt_ref[0, cc] = acc[cc * tk:(cc + 1) * tk, :].T.astype(BF16)
    nq_ref[0] = group(3).astype(BF16)
    nk_ref[0] = group(4).astype(BF16)
    nv_ref[0] = group(5).astype(BF16)


def _proj(x, mod, w_in, cos_t, sin_t, *, slot):
    b, t, d = x.shape
    tm, tk = DIFF_TQ, DIFF_TK
    kern = functools.partial(_proj_kernel, slot=slot, q_scale=DIFF_HEAD_DIM ** -0.5 * LOG2E, tk=tk)
    row_spec = pl.BlockSpec((1, tm, GROUP_COLS), lambda bb, i: (bb, i, 0))
    row_shape = jax.ShapeDtypeStruct((b, t, GROUP_COLS), BF16)
    return pl.pallas_call(
        kern,
        grid=(b, t // tm),
        in_specs=[
            pl.BlockSpec((1, tm, d), lambda bb, i: (bb, i, 0)),
            pl.BlockSpec((1, 9, d), lambda bb, i: (bb, 0, 0)),
            pl.BlockSpec((d, NUM_GROUPS * GROUP_COLS), lambda bb, i: (0, 0),
                         pipeline_mode=pl.Buffered(1)),
            pl.BlockSpec((tm, HEAD_COLS), lambda bb, i: (i, 0)),
            pl.BlockSpec((tm, HEAD_COLS), lambda bb, i: (i, 0)),
        ],
        out_specs=[
            pl.BlockSpec((1, 1, GROUP_COLS, tm), lambda bb, i: (bb, i, 0, 0)),
            row_spec,
            pl.BlockSpec((1, tm // tk, GROUP_COLS, tk), lambda bb, i: (bb, i, 0, 0)),
            row_spec, row_spec, row_spec,
        ],
        out_shape=[
            jax.ShapeDtypeStruct((b, t // tm, GROUP_COLS, tm), BF16),
            row_shape,
            jax.ShapeDtypeStruct((b, t // tk, GROUP_COLS, tk), BF16),
            row_shape, row_shape, row_shape,
        ],
        compiler_params=_params("parallel", "parallel"),
        name="proj",
    )(x, mod, w_in, cos_t, sin_t)


def _diff_kernel(qt_ref, k_ref, vt_ref, lq1_ref, lk1_ref, lq2_ref, lk2_ref, g_ref, o_ref,
                 qz_ref, st_ref, m_ref, l_ref, acc_ref, *, tq, tk, cw, nk, spt, lam_init):
    qt = qt_ref[0, 0]
    row = lax.broadcasted_iota(jnp.int32, qt.shape, 0)
    zero = jnp.zeros_like(qt)
    qz_ref[:, 0:tq] = jnp.where(row < DIFF_HEAD_DIM, qt, zero)
    qz_ref[:, tq:2 * tq] = jnp.where(row >= DIFF_HEAD_DIM, qt, zero)
    m_ref[...] = jnp.full(m_ref.shape, -jnp.inf, F32)
    l_ref[...] = jnp.zeros(l_ref.shape, F32)
    acc_ref[...] = jnp.zeros(acc_ref.shape, F32)

    def keys(j):
        return k_ref[0, pl.ds(pl.multiple_of(j * tk, tk), tk), :]

    def scores(k, slot, cols):
        st_ref[slot, :, cols] = jnp.dot(k, qz_ref[:, cols], preferred_element_type=F32)

    def step(j, slot, j_next):
        vt = vt_ref[0, j]
        k_next = None if j_next is None else keys(j_next)
        for c in range(2 * tq // cw):
            cols = slice(c * cw, (c + 1) * cw)
            if k_next is not None:
                scores(k_next, 1 - slot, cols)
            st = st_ref[slot, :, cols]
            m_old = m_ref[:, cols]
            m_new = jnp.maximum(m_old, jnp.max(st, axis=0, keepdims=True))
            alpha = jnp.exp2(m_old - m_new)
            p = jnp.exp2(st - m_new)
            l_ref[:, cols] = alpha * l_ref[:, cols] + jnp.sum(p, axis=0, keepdims=True)
            acc_ref[:, cols] = alpha * acc_ref[:, cols] + jnp.dot(
                vt, p.astype(BF16), preferred_element_type=F32)
            m_ref[:, cols] = m_new

    def body(jj, carry):
        for i in range(spt):
            step(spt * jj + i, i % 2, spt * jj + i + 1)
        return carry

    k0 = keys(0)
    for c in range(2 * tq // cw):
        scores(k0, 0, slice(c * cw, (c + 1) * cw))
    lax.fori_loop(0, nk // spt - 1, body, 0)
    for j in range(nk - spt, nk):
        step(j, j % 2, j + 1 if j + 1 < nk else None)

    lam = (jnp.exp(jnp.sum(lq1_ref[...] * lk1_ref[...], axis=1, keepdims=True))
           - jnp.exp(jnp.sum(lq2_ref[...] * lk2_ref[...], axis=1, keepdims=True))
           + lam_init)
    ot = acc_ref[...] / l_ref[...]
    a = (ot[:, 0:tq] - lam * ot[:, tq:2 * tq]).T
    a = a * lax.rsqrt(jnp.mean(a * a, axis=-1, keepdims=True) + RMS_EPS)
    a = a * g_ref[...] * (1.0 - lam_init)
    o_ref[0] = a.astype(BF16)


def _diff_attention(qt, k, vt, lq1, lk1, lq2, lk2, norm_g, *, lam_init):
    b, t, _ = k.shape
    tq, tk, cw = DIFF_TQ, DIFF_TK, DIFF_CW
    nk = t // tk
    spt = min(DIFF_STEPS_PER_TRIP, nk)
    assert spt % 2 == 0 and nk % spt == 0
    kern = functools.partial(_diff_kernel, tq=tq, tk=tk, cw=cw, nk=nk, spt=spt, lam_init=lam_init)
    vec = lambda n: pl.BlockSpec((1, n), lambda bb, h, i: (0, 0))
    return pl.pallas_call(
        kern,
        grid=(b, DIFF_HEADS, t // tq),
        in_specs=[
            pl.BlockSpec((1, 1, HEAD_COLS, tq), lambda bb, h, i: (bb, i, h, 0)),
            pl.BlockSpec((1, t, HEAD_COLS), lambda bb, h, i: (bb, 0, h)),
            pl.BlockSpec((1, t // tk, HEAD_COLS, tk), lambda bb, h, i: (bb, 0, h, 0)),
            vec(DIFF_HEAD_DIM), vec(DIFF_HEAD_DIM), vec(DIFF_HEAD_DIM), vec(DIFF_HEAD_DIM),
            vec(HEAD_COLS),
        ],
        out_specs=pl.BlockSpec((1, tq, HEAD_COLS), lambda bb, h, i: (bb, i, h)),
        out_shape=jax.ShapeDtypeStruct((b, t, DIFF_HEADS * HEAD_COLS), BF16),
        scratch_shapes=[
            pltpu.VMEM((HEAD_COLS, 2 * tq), BF16),
            pltpu.VMEM((2, tk, 2 * tq), F32),
            pltpu.VMEM((1, 2 * tq), F32),
            pltpu.VMEM((1, 2 * tq), F32),
            pltpu.VMEM((HEAD_COLS, 2 * tq), F32),
        ],
        compiler_params=_params("parallel", "parallel", "arbitrary"),
        name="diff_attn",
    )(qt, k, vt, lq1.reshape(1, -1), lk1.reshape(1, -1), lq2.reshape(1, -1),
      lk2.reshape(1, -1), norm_g.reshape(1, -1))


def _na_bias_table(rpb):
    heads = rpb.shape[0]
    qc = jnp.arange(GRID_W)
    kc = jnp.arange(GRID_W)
    cs = jnp.clip(qc - NA_KW // 2, 0, GRID_W - NA_KW)
    col_valid = (kc[None, :] >= cs[:, None]) & (kc[None, :] < cs[:, None] + NA_KW)
    delta = kc[None, :] - qc[:, None] + (NA_KW - 1)
    sel = delta[:, :, None] == jnp.arange(2 * NA_KW - 1)
    col_exp = jnp.sum(jnp.where(sel[None, None], rpb.astype(F32)[:, :, None, None, :], 0.0), axis=-1)
    col_exp = jnp.where(col_valid[None, None], col_exp, -jnp.inf)
    masked = jnp.full((heads, GRID_W, GRID_W), -jnp.inf, F32)
    tables = []
    for win_lo, off0 in ((0, 0), (None, -(NA_KH // 2)), (NA_WROWS - NA_KH, -NA_KH)):
        q_rows = []
        for ql in range(NA_QROWS):
            tiles = []
            for kl in range(NA_WROWS):
                off = kl - ql + off0
                if win_lo is None:
                    valid = -(NA_KH // 2) <= off < NA_KH // 2
                else:
                    valid = win_lo <= kl < win_lo + NA_KH
                tiles.append(col_exp[:, off + NA_KH - 1] if valid else masked)
            q_rows.append(jnp.stack(tiles, axis=2))
        tables.append(jnp.stack(q_rows, axis=1).reshape(heads, NA_QTOK, NA_WTOK))
    return jnp.stack(tables)


def _na_kernel(q_ref, k0_ref, k1_ref, k2_ref, v0_ref, v1_ref, v2_ref, bias_ref, o_ref, *, scale):
    k_refs = (k0_ref, k1_ref, k2_ref)
    v_refs = (v0_ref, v1_ref, v2_ref)
    wb = NA_WTOK // 3
    for h in range(NA_HEADS):
        cols = slice(h * NA_HEAD_DIM, (h + 1) * NA_HEAD_DIM)
        q = q_ref[0, :, cols]
        s = [lax.dot_general(q, kr[0, :, cols], (((1,), (1,)), ((), ())),
                             preferred_element_type=F32) for kr in k_refs]
        s = [s[w] * scale + bias_ref[0, h, :, w * wb:(w + 1) * wb] for w in range(3)]
        m = jnp.maximum(jnp.maximum(jnp.max(s[0], axis=1, keepdims=True),
                                    jnp.max(s[1], axis=1, keepdims=True)),
                        jnp.max(s[2], axis=1, keepdims=True))
        p = [jnp.exp(sw - m) for sw in s]
        l = (jnp.sum(p[0], axis=1, keepdims=True) + jnp.sum(p[1], axis=1, keepdims=True)
             + jnp.sum(p[2], axis=1, keepdims=True))
        o = (jnp.dot(p[0].astype(BF16), v_refs[0][0, :, cols], preferred_element_type=F32)
             + jnp.dot(p[1].astype(BF16), v_refs[1][0, :, cols], preferred_element_type=F32)
             + jnp.dot(p[2].astype(BF16), v_refs[2][0, :, cols], preferred_element_type=F32))
        o_ref[0, :, cols] = (o / l).astype(BF16)


def _na_attention(nq, nk, nv, bias):
    b, t, _ = nq.shape
    nblk = t // NA_QTOK
    assert nblk >= 3
    win = lambda i: jnp.clip(i - 1, 0, nblk - 3)
    pat = lambda i: jnp.where(i == 0, 0, jnp.where(i == nblk - 1, 2, 1))
    blk = (1, NA_QTOK, GROUP_COLS)
    kv_spec = lambda w: pl.BlockSpec(blk, lambda bb, i: (bb, win(i) + w, 0))
    kern = functools.partial(_na_kernel, scale=NA_HEAD_DIM ** -0.5)
    return pl.pallas_call(
        kern,
        grid=(b, nblk),
        in_specs=[
            pl.BlockSpec(blk, lambda bb, i: (bb, i, 0)),
            kv_spec(0), kv_spec(1), kv_spec(2),
            kv_spec(0), kv_spec(1), kv_spec(2),
            pl.BlockSpec((1, NA_HEADS, NA_QTOK, NA_WTOK), lambda bb, i: (pat(i), 0, 0, 0)),
        ],
        out_specs=pl.BlockSpec(blk, lambda bb, i: (bb, i, 0)),
        out_shape=jax.ShapeDtypeStruct((b, t, GROUP_COLS), BF16),
        compiler_params=_params("parallel", "arbitrary"),
        name="na_attn",
    )(nq, nk, nk, nk, nv, nv, nv, bias)


def _out_kernel(x_ref, od_ref, on_ref, mod_ref, w_ref, lng_ref, lnb_ref, o_ref, *, slot, alpha):
    half = od_ref.shape[-1]
    a = (jnp.dot(od_ref[0], w_ref[0:half, :], preferred_element_type=F32)
         + jnp.dot(on_ref[0], w_ref[half:2 * half, :], preferred_element_type=F32))
    gate = mod_ref[0, 3 * slot + 2:3 * slot + 3, :]
    y = alpha * x_ref[0] + (1 + gate) * a
    o_ref[0] = _layer_norm(y, lng_ref[...], lnb_ref[...])


def _out_proj(x, o_diff, o_na, mod, w_out, ln_g, ln_b, *, slot, alpha, tm=512):
    b, t, d = x.shape
    half = o_diff.shape[-1]
    kern = functools.partial(_out_kernel, slot=slot, alpha=alpha)
    return pl.pallas_call(
        kern,
        grid=(b, t // tm),
        in_specs=[
            pl.BlockSpec((1, tm, d), lambda bb, i: (bb, i, 0)),
            pl.BlockSpec((1, tm, half), lambda bb, i: (bb, i, 0)),
            pl.BlockSpec((1, tm, half), lambda bb, i: (bb, i, 0)),
            pl.BlockSpec((1, 9, d), lambda bb, i: (bb, 0, 0)),
            pl.BlockSpec((2 * half, d), lambda bb, i: (0, 0)),
            pl.BlockSpec((1, d), lambda bb, i: (0, 0)),
            pl.BlockSpec((1, d), lambda bb, i: (0, 0)),
        ],
        out_specs=pl.BlockSpec((1, tm, d), lambda bb, i: (bb, i, 0)),
        out_shape=jax.ShapeDtypeStruct((b, t, d), F32),
        compiler_params=_params("parallel", "parallel"),
        name="out_proj",
    )(x, o_diff, o_na, mod, w_out, ln_g.reshape(1, d), ln_b.reshape(1, d))


def _rotary_tables(t):
    half = DIFF_HEAD_DIM // 2
    inv_freq = ROPE_THETA ** (-jnp.arange(0, DIFF_HEAD_DIM, 2, dtype=F32) / DIFF_HEAD_DIM)
    ang = jnp.arange(t, dtype=F32)[:, None] * inv_freq[None, :]
    cos = jnp.cos(ang)
    sin = jnp.sin(ang)
    reps = HEAD_COLS // half
    cos_t = jnp.concatenate([cos] * reps, axis=-1)
    sin_t = jnp.concatenate([-sin, sin] * (reps // 2), axis=-1)
    return cos_t, sin_t


def kernel(x_prompt, x_sample, c_prompt, c_sample, w_ada, b_ada, ln_g, ln_b, ffn1_w_gu, ffn1_w_down, w_in, w_out, lam_q1, lam_k1, lam_q2, lam_k2, diff_norm_g, rpb, ffn2_w_gu, ffn2_w_down):
    depth, d, _ = w_ada.shape
    alpha = (2.0 * depth) ** 0.25

    groups = ((x_prompt, c_prompt), (x_sample, c_sample))
    n_cond = sum(c.shape[0] for _, c in groups)
    rows = -(-n_cond // 8) * 8
    c_all = jnp.concatenate([c for _, c in groups] + [jnp.zeros((rows - n_cond, d), F32)], axis=0)
    mod_all = _ada(c_all, w_ada, b_ada)

    w1gu = ffn1_w_gu.astype(BF16)
    w1d = ffn1_w_down.astype(BF16)
    w2gu = ffn2_w_gu.astype(BF16)
    w2d = ffn2_w_down.astype(BF16)
    win = w_in.astype(BF16)
    wout = w_out.astype(BF16)
    na_bias = [_na_bias_table(rpb[l]) for l in range(depth)]

    outs = []
    row0 = 0
    for x, c in groups:
        b, t, _ = x.shape
        cos_t, sin_t = _rotary_tables(t)
        for l in range(depth):
            mod = mod_all[l, row0:row0 + b].reshape(b, 9, d)
            lam_init = 0.8 - 0.6 * math.exp(-0.3 * l)
            x = _ffn(x, mod, w1gu[l], w1d[l], ln_g[l, 0], ln_b[l, 0], slot=0, alpha=alpha)
            qt, k, vt, nq, nk, nv = _proj(x, mod, win[l], cos_t, sin_t, slot=1)
            o_diff = _diff_attention(qt, k, vt, lam_q1[l], lam_k1[l], lam_q2[l], lam_k2[l],
                                     diff_norm_g[l], lam_init=lam_init)
            o_na = _na_attention(nq, nk, nv, na_bias[l])
            x = _out_proj(x, o_diff, o_na, mod, wout[l], ln_g[l, 1], ln_b[l, 1],
                          slot=1, alpha=alpha)
            x = _ffn(x, mod, w2gu[l], w2d[l], ln_g[l, 2], ln_b[l, 2], slot=2, alpha=alpha)
        outs.append(x)
        row0 += b
    return tuple(outs)
```

```python
import functools
import math

import jax
import jax.numpy as jnp
from jax import lax
from jax.experimental import pallas as pl
from jax.experimental.pallas import tpu as pltpu

DIFF_HEADS = 8
DIFF_HEAD_DIM = 64
NA_HEADS = 8
NA_HEAD_DIM = 128
HEAD_COLS = 128
GROUP_COLS = 1024
NUM_GROUPS = 6
GRID_W = 64
NA_KH = 8
NA_KW = 16
ROPE_THETA = 10000.0
LN_EPS = 1e-5
RMS_EPS = 1e-5

PROJ_TM = 512
DIFF_TQ = 512
DIFF_TK = 512
DIFF_CW = 256
DIFF_STEPS_PER_TRIP = 8

NA_QROWS = 4
NA_WROWS = 12
NA_QTOK = NA_QROWS * GRID_W
NA_WTOK = NA_WROWS * GRID_W

VMEM_LIMIT = 56 * 1024 * 1024

F32 = jnp.float32
BF16 = jnp.bfloat16
LOG2E = math.log2(math.e)


def _params(*sem):
    return pltpu.CompilerParams(dimension_semantics=sem, vmem_limit_bytes=VMEM_LIMIT)


def _layer_norm(y, g, b):
    mu = jnp.mean(y, axis=-1, keepdims=True)
    d = y - mu
    var = jnp.mean(d * d, axis=-1, keepdims=True)
    return d * lax.rsqrt(var + LN_EPS) * g + b


def _ada_kernel(c_ref, w_ref, b_ref, o_ref):
    c = c_ref[...]
    a = c * jax.nn.sigmoid(c)
    o_ref[0] = jnp.dot(a, w_ref[0], preferred_element_type=F32) + b_ref[0]


def _ada(c_pad, w_ada, b_ada, tn=1024):
    depth, d, n = w_ada.shape
    rows = c_pad.shape[0]
    return pl.pallas_call(
        _ada_kernel,
        grid=(depth, n // tn),
        in_specs=[
            pl.BlockSpec((rows, d), lambda l, j: (0, 0)),
            pl.BlockSpec((1, d, tn), lambda l, j: (l, 0, j)),
            pl.BlockSpec((1, 1, tn), lambda l, j: (l, 0, j)),
        ],
        out_specs=pl.BlockSpec((1, rows, tn), lambda l, j: (l, 0, j)),
        out_shape=jax.ShapeDtypeStruct((depth, rows, n), F32),
        compiler_params=_params("arbitrary", "arbitrary"),
        name="ada",
    )(c_pad, w_ada, b_ada.reshape(depth, 1, n))


def _ffn_kernel(x_ref, mod_ref, wg_ref, wu_ref, wd_ref, lng_ref, lnb_ref, o_ref,
                h_ref, a_ref, acc_ref, *, slot, alpha, nf):
    j = pl.program_id(2)

    def gate_up(buf):
        h = h_ref[...]
        g = jnp.dot(h, wg_ref[...], preferred_element_type=F32)
        u = jnp.dot(h, wu_ref[...], preferred_element_type=F32)
        a_ref[buf] = (g * jax.nn.sigmoid(g) * u).astype(BF16)

    def down(buf):
        acc_ref[...] += jnp.dot(a_ref[buf], wd_ref[...], preferred_element_type=F32)

    @pl.when(j == 0)
    def _():
        shift = mod_ref[0, 3 * slot:3 * slot + 1, :]
        scale = mod_ref[0, 3 * slot + 1:3 * slot + 2, :]
        h_ref[...] = (x_ref[0] * (1 + scale) + shift).astype(BF16)
        acc_ref[...] = jnp.zeros(acc_ref.shape, F32)
        gate_up(0)

    for parity in (0, 1):
        @pl.when((j > 0) & (j < nf) & (j % 2 == parity))
        def _():
            gate_up(parity)
            down(1 - parity)

    @pl.when(j == nf)
    def _():
        down((nf - 1) % 2)
        gate = mod_ref[0, 3 * slot + 2:3 * slot + 3, :]
        y = alpha * x_ref[0] + ((1 + gate) * 0.5) * acc_ref[...]
        o_ref[0] = _layer_norm(y, lng_ref[...], lnb_ref[...])


def _ffn(x, mod, w_gu, w_down, ln_g, ln_b, *, layer, slot, alpha, tm=512, tf=512):
    b, t, d = x.shape
    dff = w_down.shape[1]
    nf = dff // tf
    kern = functools.partial(_ffn_kernel, slot=slot, alpha=alpha, nf=nf)
    last = nf - 1
    return pl.pallas_call(
        kern,
        grid=(b, t // tm, nf + 1),
        in_specs=[
            pl.BlockSpec((1, tm, d), lambda bb, i, j: (bb, i, 0)),
            pl.BlockSpec((1, 9, d), lambda bb, i, j: (bb, 0, 0)),
            pl.BlockSpec((None, d, tf), lambda bb, i, j: (layer, 0, jnp.minimum(j, last))),
            pl.BlockSpec((None, d, tf), lambda bb, i, j: (layer, 0, jnp.minimum(j, last) + nf)),
            pl.BlockSpec((None, tf, d), lambda bb, i, j: (layer, jnp.maximum(j - 1, 0), 0)),
            pl.BlockSpec((1, d), lambda bb, i, j: (0, 0)),
            pl.BlockSpec((1, d), lambda bb, i, j: (0, 0)),
        ],
        out_specs=pl.BlockSpec((1, tm, d), lambda bb, i, j: (bb, i, 0)),
        out_shape=jax.ShapeDtypeStruct((b, t, d), F32),
        scratch_shapes=[pltpu.VMEM((tm, d), BF16), pltpu.VMEM((2, tm, tf), BF16),
                        pltpu.VMEM((tm, d), F32)],
        compiler_params=_params("parallel", "parallel", "arbitrary"),
        name="ffn",
    )(x, mod, w_gu, w_gu, w_down, ln_g.reshape(1, d), ln_b.reshape(1, d))


def _proj_kernel(x_ref, mod_ref, w_ref, cos_ref, sin_ref,
                 qt_ref, k_ref, vt_ref, nq_ref, nk_ref, nv_ref, *, slot, q_scale, tk):
    shift = mod_ref[0, 3 * slot:3 * slot + 1, :]
    scale = mod_ref[0, 3 * slot + 1:3 * slot + 2, :]
    h = (x_ref[0] * (1 + scale) + shift).astype(BF16)
    tm = h.shape[0]

    def group(g):
        return jnp.dot(h, w_ref[:, g * GROUP_COLS:(g + 1) * GROUP_COLS],
                       preferred_element_type=F32)

    cos = cos_ref[...]
    sin = sin_ref[...]
    lane = lax.broadcasted_iota(jnp.int32, cos.shape, 1)
    first_half = (lane & (DIFF_HEAD_DIM - 1)) < (DIFF_HEAD_DIM // 2)

    def rotary(xs):
        fwd = pltpu.roll(xs, HEAD_COLS - DIFF_HEAD_DIM // 2, axis=1)
        bwd = pltpu.roll(xs, DIFF_HEAD_DIM // 2, axis=1)
        return xs * cos + jnp.where(first_half, fwd, bwd) * sin

    acc = group(0)
    for hh in range(DIFF_HEADS):
        cols = slice(hh * HEAD_COLS, (hh + 1) * HEAD_COLS)
        qt_ref[0, 0, cols, :] = (rotary(acc[:, cols]) * q_scale).T.astype(BF16)
    acc = group(1)
    for hh in range(DIFF_HEADS):
        cols = slice(hh * HEAD_COLS, (hh + 1) * HEAD_COLS)
        k_ref[0, :, cols] = rotary(acc[:, cols]).astype(BF16)
    acc = group(2)
    piece = min(tm, tk)
    for cc in range(tm // piece):
        vt_ref[0, cc] = acc[cc * piece:(cc + 1) * piece, :].T.astype(BF16)
    nq_ref[0] = group(3).astype(BF16)
    nk_ref[0] = group(4).astype(BF16)
    nv_ref[0] = group(5).astype(BF16)


def _proj(x, mod, w_in, cos_t, sin_t, *, layer, slot):
    b, t, d = x.shape
    tm, tk = PROJ_TM, DIFF_TK
    kern = functools.partial(_proj_kernel, slot=slot, q_scale=DIFF_HEAD_DIM ** -0.5 * LOG2E, tk=tk)
    if tk <= tm:
        vt_spec = pl.BlockSpec((1, tm // tk, GROUP_COLS, tk), lambda bb, i: (bb, i, 0, 0))
    else:
        r = tk // tm
        vt_spec = pl.BlockSpec((1, 1, GROUP_COLS, tm), lambda bb, i: (bb, i // r, 0, i % r))
    row_spec = pl.BlockSpec((1, tm, GROUP_COLS), lambda bb, i: (bb, i, 0))
    row_shape = jax.ShapeDtypeStruct((b, t, GROUP_COLS), BF16)
    return pl.pallas_call(
        kern,
        grid=(b, t // tm),
        in_specs=[
            pl.BlockSpec((1, tm, d), lambda bb, i: (bb, i, 0)),
            pl.BlockSpec((1, 9, d), lambda bb, i: (bb, 0, 0)),
            pl.BlockSpec((None, d, NUM_GROUPS * GROUP_COLS), lambda bb, i: (layer, 0, 0),
                         pipeline_mode=pl.Buffered(1)),
            pl.BlockSpec((tm, HEAD_COLS), lambda bb, i: (i, 0)),
            pl.BlockSpec((tm, HEAD_COLS), lambda bb, i: (i, 0)),
        ],
        out_specs=[
            pl.BlockSpec((1, 1, GROUP_COLS, tm), lambda bb, i: (bb, i, 0, 0)),
            row_spec,
            vt_spec,
            row_spec, row_spec, row_spec,
        ],
        out_shape=[
            jax.ShapeDtypeStruct((b, t // tm, GROUP_COLS, tm), BF16),
            row_shape,
            jax.ShapeDtypeStruct((b, t // tk, GROUP_COLS, tk), BF16),
            row_shape, row_shape, row_shape,
        ],
        compiler_params=_params("parallel", "parallel"),
        name="proj",
    )(x, mod, w_in, cos_t, sin_t)


def _diff_kernel(qt_ref, k_ref, vt_ref, lq1_ref, lk1_ref, lq2_ref, lk2_ref, g_ref, o_ref,
                 qz_ref, st_ref, m_ref, l_ref, acc_ref, *, tq, tk, cw, nk, spt, lam_init):
    pw = qt_ref.shape[-1]
    for piece in range(tq // pw):
        qt = qt_ref[0, piece]
        row = lax.broadcasted_iota(jnp.int32, qt.shape, 0)
        zero = jnp.zeros_like(qt)
        lo = piece * pw
        qz_ref[:, lo:lo + pw] = jnp.where(row < DIFF_HEAD_DIM, qt, zero)
        qz_ref[:, tq + lo:tq + lo + pw] = jnp.where(row >= DIFF_HEAD_DIM, qt, zero)
    m_ref[...] = jnp.full(m_ref.shape, -jnp.inf, F32)
    l_ref[...] = jnp.zeros(l_ref.shape, F32)
    acc_ref[...] = jnp.zeros(acc_ref.shape, F32)

    def keys(j):
        return k_ref[0, pl.ds(pl.multiple_of(j * tk, tk), tk), :]

    def scores(k, slot, cols):
        st_ref[slot, :, cols] = jnp.dot(k, qz_ref[:, cols], preferred_element_type=F32)

    def step(j, slot, j_next):
        vt = vt_ref[0, j]
        k_next = None if j_next is None else keys(j_next)
        for c in range(2 * tq // cw):
            cols = slice(c * cw, (c + 1) * cw)
            if k_next is not None:
                scores(k_next, 1 - slot, cols)
            st = st_ref[slot, :, cols]
            m_old = m_ref[:, cols]
            m_new = jnp.maximum(m_old, jnp.max(st, axis=0, keepdims=True))
            alpha = jnp.exp2(m_old - m_new)
            p = jnp.exp2(st - m_new)
            l_ref[:, cols] = alpha * l_ref[:, cols] + jnp.sum(p, axis=0, keepdims=True)
            acc_ref[:, cols] = alpha * acc_ref[:, cols] + jnp.dot(
                vt, p.astype(BF16), preferred_element_type=F32)
            m_ref[:, cols] = m_new

    def body(jj, carry):
        for i in range(spt):
            step(spt * jj + i, i % 2, spt * jj + i + 1)
        return carry

    k0 = keys(0)
    for c in range(2 * tq // cw):
        scores(k0, 0, slice(c * cw, (c + 1) * cw))
    lax.fori_loop(0, nk // spt - 1, body, 0)
    for j in range(nk - spt, nk):
        step(j, j % 2, j + 1 if j + 1 < nk else None)

    lam = (jnp.exp(jnp.sum(lq1_ref[...] * lk1_ref[...], axis=1, keepdims=True))
           - jnp.exp(jnp.sum(lq2_ref[...] * lk2_ref[...], axis=1, keepdims=True))
           + lam_init)
    ot = acc_ref[...] / l_ref[...]
    a = (ot[:, 0:tq] - lam * ot[:, tq:2 * tq]).T
    a = a * lax.rsqrt(jnp.mean(a * a, axis=-1, keepdims=True) + RMS_EPS)
    a = a * g_ref[...] * (1.0 - lam_init)
    o_ref[0] = a.astype(BF16)


def _diff_attention(qt, k, vt, lq1, lk1, lq2, lk2, norm_g, *, lam_init):
    b, t, _ = k.shape
    pw = qt.shape[-1]
    tq, tk, cw = DIFF_TQ, DIFF_TK, DIFF_CW
    nk = t // tk
    spt = min(DIFF_STEPS_PER_TRIP, nk)
    assert spt % 2 == 0 and nk % spt == 0
    kern = functools.partial(_diff_kernel, tq=tq, tk=tk, cw=cw, nk=nk, spt=spt, lam_init=lam_init)
    vec = lambda n: pl.BlockSpec((1, n), lambda bb, h, i: (0, 0))
    return pl.pallas_call(
        kern,
        grid=(b, DIFF_HEADS, t // tq),
        in_specs=[
            pl.BlockSpec((1, tq // pw, HEAD_COLS, pw), lambda bb, h, i: (bb, i, h, 0)),
            pl.BlockSpec((1, t, HEAD_COLS), lambda bb, h, i: (bb, 0, h)),
            pl.BlockSpec((1, t // tk, HEAD_COLS, tk), lambda bb, h, i: (bb, 0, h, 0)),
            vec(DIFF_HEAD_DIM), vec(DIFF_HEAD_DIM), vec(DIFF_HEAD_DIM), vec(DIFF_HEAD_DIM),
            vec(HEAD_COLS),
        ],
        out_specs=pl.BlockSpec((1, tq, HEAD_COLS), lambda bb, h, i: (bb, i, h)),
        out_shape=jax.ShapeDtypeStruct((b, t, DIFF_HEADS * HEAD_COLS), BF16),
        scratch_shapes=[
            pltpu.VMEM((HEAD_COLS, 2 * tq), BF16),
            pltpu.VMEM((2, tk, 2 * tq), F32),
            pltpu.VMEM((1, 2 * tq), F32),
            pltpu.VMEM((1, 2 * tq), F32),
            pltpu.VMEM((HEAD_COLS, 2 * tq), F32),
        ],
        compiler_params=_params("parallel", "parallel", "arbitrary"),
        name="diff_attn",
    )(qt, k, vt, lq1.reshape(1, -1), lk1.reshape(1, -1), lq2.reshape(1, -1),
      lk2.reshape(1, -1), norm_g.reshape(1, -1))


def _na_bias_table(rpb):
    heads = rpb.shape[0]
    qc = jnp.arange(GRID_W)
    kc = jnp.arange(GRID_W)
    cs = jnp.clip(qc - NA_KW // 2, 0, GRID_W - NA_KW)
    col_valid = (kc[None, :] >= cs[:, None]) & (kc[None, :] < cs[:, None] + NA_KW)
    delta = kc[None, :] - qc[:, None] + (NA_KW - 1)
    sel = delta[:, :, None] == jnp.arange(2 * NA_KW - 1)
    col_exp = jnp.sum(jnp.where(sel[None, None], rpb.astype(F32)[:, :, None, None, :], 0.0), axis=-1)
    col_exp = jnp.where(col_valid[None, None], col_exp * LOG2E, -jnp.inf)
    masked = jnp.full((heads, GRID_W, GRID_W), -jnp.inf, F32)
    tables = []
    for win_lo, off0 in ((0, 0), (None, -(NA_KH // 2)), (NA_WROWS - NA_KH, -NA_KH)):
        q_rows = []
        for ql in range(NA_QROWS):
            tiles = []
            for kl in range(NA_WROWS):
                off = kl - ql + off0
                if win_lo is None:
                    valid = -(NA_KH // 2) <= off < NA_KH // 2
                else:
                    valid = win_lo <= kl < win_lo + NA_KH
                tiles.append(col_exp[:, off + NA_KH - 1] if valid else masked)
            q_rows.append(jnp.stack(tiles, axis=2))
        tables.append(jnp.stack(q_rows, axis=1).reshape(heads, NA_QTOK, NA_WTOK))
    return jnp.stack(tables)


def _na_kernel(q_ref, k0_ref, k1_ref, k2_ref, v0_ref, v1_ref, v2_ref, bias_ref, o_ref, *, scale):
    k_refs = (k0_ref, k1_ref, k2_ref)
    v_refs = (v0_ref, v1_ref, v2_ref)
    wb = NA_WTOK // 3
    for h in range(NA_HEADS):
        cols = slice(h * NA_HEAD_DIM, (h + 1) * NA_HEAD_DIM)
        q = q_ref[0, :, cols]
        s = [lax.dot_general(q, kr[0, :, cols], (((1,), (1,)), ((), ())),
                             preferred_element_type=F32) for kr in k_refs]
        s = [s[w] * scale + bias_ref[0, h, :, w * wb:(w + 1) * wb] for w in range(3)]
        m = jnp.maximum(jnp.maximum(jnp.max(s[0], axis=1, keepdims=True),
                                    jnp.max(s[1], axis=1, keepdims=True)),
                        jnp.max(s[2], axis=1, keepdims=True))
        p = [jnp.exp2(sw - m) for sw in s]
        l = (jnp.sum(p[0], axis=1, keepdims=True) + jnp.sum(p[1], axis=1, keepdims=True)
             + jnp.sum(p[2], axis=1, keepdims=True))
        o = (jnp.dot(p[0].astype(BF16), v_refs[0][0, :, cols], preferred_element_type=F32)
             + jnp.dot(p[1].astype(BF16), v_refs[1][0, :, cols], preferred_element_type=F32)
             + jnp.dot(p[2].astype(BF16), v_refs[2][0, :, cols], preferred_element_type=F32))
        o_ref[0, :, cols] = (o / l).astype(BF16)


def _na_attention(nq, nk, nv, bias):
    b, t, _ = nq.shape
    nblk = t // NA_QTOK
    assert nblk >= 3
    win = lambda i: jnp.clip(i - 1, 0, nblk - 3)
    pat = lambda i: jnp.where(i == 0, 0, jnp.where(i == nblk - 1, 2, 1))
    blk = (1, NA_QTOK, GROUP_COLS)
    kv_spec = lambda w: pl.BlockSpec(blk, lambda bb, i: (bb, win(i) + w, 0))
    kern = functools.partial(_na_kernel, scale=NA_HEAD_DIM ** -0.5 * LOG2E)
    return pl.pallas_call(
        kern,
        grid=(b, nblk),
        in_specs=[
            pl.BlockSpec(blk, lambda bb, i: (bb, i, 0)),
            kv_spec(0), kv_spec(1), kv_spec(2),
            kv_spec(0), kv_spec(1), kv_spec(2),
            pl.BlockSpec((1, NA_HEADS, NA_QTOK, NA_WTOK), lambda bb, i: (pat(i), 0, 0, 0)),
        ],
        out_specs=pl.BlockSpec(blk, lambda bb, i: (bb, i, 0)),
        out_shape=jax.ShapeDtypeStruct((b, t, GROUP_COLS), BF16),
        compiler_params=_params("parallel", "arbitrary"),
        name="na_attn",
    )(nq, nk, nk, nk, nv, nv, nv, bias)


def _out_kernel(x_ref, od_ref, on_ref, mod_ref, w_ref, lng_ref, lnb_ref, o_ref, *, slot, alpha):
    half = od_ref.shape[-1]
    a = (jnp.dot(od_ref[0], w_ref[0:half, :], preferred_element_type=F32)
         + jnp.dot(on_ref[0], w_ref[half:2 * half, :], preferred_element_type=F32))
    gate = mod_ref[0, 3 * slot + 2:3 * slot + 3, :]
    y = alpha * x_ref[0] + (1 + gate) * a
    o_ref[0] = _layer_norm(y, lng_ref[...], lnb_ref[...])


def _out_proj(x, o_diff, o_na, mod, w_out, ln_g, ln_b, *, layer, slot, alpha, tm=512):
    b, t, d = x.shape
    half = o_diff.shape[-1]
    kern = functools.partial(_out_kernel, slot=slot, alpha=alpha)
    return pl.pallas_call(
        kern,
        grid=(b, t // tm),
        in_specs=[
            pl.BlockSpec((1, tm, d), lambda bb, i: (bb, i, 0)),
            pl.BlockSpec((1, tm, half), lambda bb, i: (bb, i, 0)),
            pl.BlockSpec((1, tm, half), lambda bb, i: (bb, i, 0)),
            pl.BlockSpec((1, 9, d), lambda bb, i: (bb, 0, 0)),
            pl.BlockSpec((None, 2 * half, d), lambda bb, i: (layer, 0, 0)),
            pl.BlockSpec((1, d), lambda bb, i: (0, 0)),
            pl.BlockSpec((1, d), lambda bb, i: (0, 0)),
        ],
        out_specs=pl.BlockSpec((1, tm, d), lambda bb, i: (bb, i, 0)),
        out_shape=jax.ShapeDtypeStruct((b, t, d), F32),
        compiler_params=_params("parallel", "parallel"),
        name="out_proj",
    )(x, o_diff, o_na, mod, w_out, ln_g.reshape(1, d), ln_b.reshape(1, d))


def _rotary_tables(t):
    half = DIFF_HEAD_DIM // 2
    inv_freq = ROPE_THETA ** (-jnp.arange(0, DIFF_HEAD_DIM, 2, dtype=F32) / DIFF_HEAD_DIM)
    ang = jnp.arange(t, dtype=F32)[:, None] * inv_freq[None, :]
    cos = jnp.cos(ang)
    sin = jnp.sin(ang)
    reps = HEAD_COLS // half
    cos_t = jnp.concatenate([cos] * reps, axis=-1)
    sin_t = jnp.concatenate([-sin, sin] * (reps // 2), axis=-1)
    return cos_t, sin_t


def kernel(x_prompt, x_sample, c_prompt, c_sample, w_ada, b_ada, ln_g, ln_b, ffn1_w_gu, ffn1_w_down, w_in, w_out, lam_q1, lam_k1, lam_q2, lam_k2, diff_norm_g, rpb, ffn2_w_gu, ffn2_w_down):
    depth, d, _ = w_ada.shape
    alpha = (2.0 * depth) ** 0.25

    groups = ((x_prompt, c_prompt), (x_sample, c_sample))
    n_cond = sum(c.shape[0] for _, c in groups)
    rows = -(-n_cond // 8) * 8
    c_all = jnp.concatenate([c for _, c in groups] + [jnp.zeros((rows - n_cond, d), F32)], axis=0)
    mod_all = _ada(c_all, w_ada, b_ada)

    w1gu = ffn1_w_gu.astype(BF16)
    w1d = ffn1_w_down.astype(BF16)
    w2gu = ffn2_w_gu.astype(BF16)
    w2d = ffn2_w_down.astype(BF16)
    win = w_in.astype(BF16)
    wout = w_out.astype(BF16)
    na_bias = [_na_bias_table(rpb[l]) for l in range(depth)]

    outs = []
    row0 = 0
    for x, c in groups:
        b, t, _ = x.shape
        cos_t, sin_t = _rotary_tables(t)
        for l in range(depth):
            mod = mod_all[l, row0:row0 + b].reshape(b, 9, d)
            lam_init = 0.8 - 0.6 * math.exp(-0.3 * l)
            x = _ffn(x, mod, w1gu, w1d, ln_g[l, 0], ln_b[l, 0], layer=l, slot=0, alpha=alpha)
            qt, k, vt, nq, nk, nv = _proj(x, mod, win, cos_t, sin_t, layer=l, slot=1)
            o_diff = _diff_attention(qt, k, vt, lam_q1[l], lam_k1[l], lam_q2[l], lam_k2[l],
                                     diff_norm_g[l], lam_init=lam_init)
            o_na = _na_attention(nq, nk, nv, na_bias[l])
            x = _out_proj(x, o_diff, o_na, mod, wout, ln_g[l, 1], ln_b[l, 1],
                          layer=l, slot=1, alpha=alpha)
            x = _ffn(x, mod, w2gu, w2d, ln_g[l, 2], ln_b[l, 2], layer=l, slot=2, alpha=alpha)
        outs.append(x)
        row0 += b
    return tuple(outs)
```
